```python
import math
import jax, jax.numpy as jnp
from jax import lax
import numpy as np

D_MODEL = 2048
BATCH = 32
SEQ = 256
DEPTH = 4
DEC_BATCH = 8
DEC_SEQ = 4096
PAST_LEN = 256

GRID_W = 64
D_MIX = D_MODEL
A_WIDTH = D_MIX // 2
A_HEAD_DIM = 64
A_V_DIM = 2 * A_HEAD_DIM
A_HEADS = A_WIDTH // A_V_DIM
B_WIDTH = D_MIX // 4
B_GROUP_DIM = 128
B_GROUPS = B_WIDTH // B_GROUP_DIM
B_CHUNK = 128
C_WIDTH = D_MIX - A_WIDTH - B_WIDTH
C_GROUP_DIM = 128
C_GROUPS = C_WIDTH // C_GROUP_DIM
IN_COLS = 3 * A_WIDTH + 2 * B_WIDTH + C_WIDTH
N_EXPERTS = 32
TOP_K = 4
D_FF = D_MODEL // 2
SWIGLU_LIMIT = 7.0
SWIGLU_ALPHA = 1.702
MOE_BLOCK = 128
Q_BLOCK = 128
ROPE_THETA = 10000.0
ROPE_PAIRS_PER_AXIS = A_HEAD_DIM // 4
EPS = 1e-6

kernel_name = 'hybrid_diff_gmlp_fnet_moe_dit_step'


def rms_norm(x, g):
    xf = x.astype(jnp.float32)
    y = xf * lax.rsqrt(jnp.mean(xf * xf, axis=-1, keepdims=True) + EPS)
    return (y * g.astype(jnp.float32)).astype(x.dtype)


def axial_rope(n_tok):
    rows = n_tok // GRID_W
    row = jnp.repeat(jnp.arange(rows, dtype=jnp.float32), GRID_W)
    col = jnp.tile(jnp.arange(GRID_W, dtype=jnp.float32), rows)
    freqs = ROPE_THETA ** (-jnp.arange(ROPE_PAIRS_PER_AXIS, dtype=jnp.float32) / ROPE_PAIRS_PER_AXIS)
    ang = jnp.concatenate([row[:, None] * freqs, col[:, None] * freqs], axis=-1)
    return jnp.cos(ang), jnp.sin(ang)


def apply_rope(x, cos, sin):
    half = x.shape[-1] // 2
    x1, x2 = x[..., :half], x[..., half:]
    c = cos[None, :, None, None, :].astype(x.dtype)
    s = sin[None, :, None, None, :].astype(x.dtype)
    return jnp.concatenate([x1 * c - x2 * s, x1 * s + x2 * c], axis=-1)


def diff_attention(q, k, v, lam):
    b, nq, h, _, dh = q.shape
    nblk = nq // Q_BLOCK
    qb = q.reshape(b, nblk, Q_BLOCK, h, 2, dh).swapaxes(0, 1)
    scale = dh ** -0.5

    def block(qi):
        s = jnp.einsum('bqhmd,bkhmd->bmhqk', qi, k, preferred_element_type=jnp.float32) * scale
        p = jax.nn.softmax(s, axis=-1)
        a = p[:, 0] - lam * p[:, 1]
        return jnp.einsum('bhqk,bkhd->bqhd', a.astype(v.dtype), v)

    o = lax.map(block, qb)
    return o.swapaxes(0, 1).reshape(b, nq, h, v.shape[-1])


def spatial_gating(u, gv, sgu_w, sgu_b):
    b, n, _ = u.shape
    vc = gv.reshape(b, n // B_CHUNK, B_CHUNK, B_GROUPS, B_GROUP_DIM)
    gate = jnp.einsum('gpq,bcqgd->bcpgd', sgu_w, vc) + sgu_b.T[None, None, :, :, None]
    return u * gate.reshape(b, n, B_WIDTH)


def fourier_mix(z):
    b, n, _ = z.shape
    zf = z.astype(jnp.float32).reshape(b, n, C_GROUPS, C_GROUP_DIM)
    y = jnp.fft.fft2(zf, axes=(1, 3), norm='ortho').real
    return y.reshape(b, n, C_WIDTH).astype(z.dtype)


def moe_ffn(h, router_w, router_b, w1, b1, w2, b2):
    n_tok, d = h.shape
    logits = (h @ router_w + router_b).astype(jnp.float32)
    top_val, top_idx = lax.top_k(logits, TOP_K)
    gates = jax.nn.softmax(top_val, axis=-1)
    n_assign = n_tok * TOP_K
    e_flat = top_idx.reshape(-1)
    tok_flat = jnp.repeat(jnp.arange(n_tok, dtype=jnp.int32), TOP_K)
    g_flat = gates.reshape(-1)
    order = jnp.argsort(e_flat)
    e_s, tok_s, g_s = e_flat[order], tok_flat[order], g_flat[order]
    counts = jnp.bincount(e_flat, length=N_EXPERTS)
    starts = jnp.cumsum(counts) - counts
    padded = (counts + MOE_BLOCK - 1) // MOE_BLOCK * MOE_BLOCK
    pad_ends = jnp.cumsum(padded)
    pad_starts = pad_ends - padded
    dest = pad_starts[e_s] + jnp.arange(n_assign, dtype=jnp.int32) - starts[e_s]
    n_blocks = -(-n_assign // MOE_BLOCK) + N_EXPERTS
    n_rows = n_blocks * MOE_BLOCK
    buf_tok = jnp.full((n_rows,), n_tok, jnp.int32).at[dest].set(tok_s)
    buf_g = jnp.zeros((n_rows,), jnp.float32).at[dest].set(g_s)
    blk_expert = jnp.minimum(
        jnp.searchsorted(pad_ends, jnp.arange(n_blocks, dtype=jnp.int32) * MOE_BLOCK, side='right'),
        N_EXPERTS - 1)
    h_pad = jnp.concatenate([h, jnp.zeros((1, d), h.dtype)], axis=0)

    def block(acc, blk):
        tok_b, g_b, e = blk
        gu = h_pad[tok_b] @ w1[e] + b1[e]
        gate, up = jnp.split(gu, 2, axis=-1)
        gate = jnp.minimum(gate, SWIGLU_LIMIT)
        up = jnp.clip(up, -SWIGLU_LIMIT, SWIGLU_LIMIT)
        act = (up + 1) * gate * jax.nn.sigmoid(SWIGLU_ALPHA * gate)
        y = act @ w2[e] + b2[e]
        return acc.at[tok_b].add(y * g_b[:, None].astype(y.dtype)), None

    acc, _ = lax.scan(block, jnp.zeros((n_tok + 1, d), h.dtype),
                      (buf_tok.reshape(n_blocks, MOE_BLOCK), buf_g.reshape(n_blocks, MOE_BLOCK), blk_expert))
    return acc[:n_tok]


def trunk_layer(x, cond, rope, ctx_k, ctx_v, lam_init, w_ada, b_ada, norm_mix, norm_ffn, w_in, lam,
                subln_w, sgu_w, sgu_b, w_out, router_w, router_b, moe_w1, moe_b1, moe_w2, moe_b2):
    b, n, d = x.shape
    m = jax.nn.silu(cond) @ w_ada + b_ada
    sh1, sc1, g1, sh2, sc2, g2 = [t[:, None, :] for t in jnp.split(m, 6, axis=-1)]
    h = rms_norm(x, norm_mix) * (1 + sc1) + sh1
    proj = h @ w_in
    q, k, v, u, gv, z = jnp.split(
        proj, [A_WIDTH, 2 * A_WIDTH, 3 * A_WIDTH, 3 * A_WIDTH + B_WIDTH, 3 * A_WIDTH + 2 * B_WIDTH], axis=-1)
    q = q.reshape(b, n, A_HEADS, 2, A_HEAD_DIM)
    k = k.reshape(b, n, A_HEADS, 2, A_HEAD_DIM)
    v = v.reshape(b, n, A_HEADS, A_V_DIM)
    kv_out = (k.reshape(b, n, A_HEADS, 2 * A_HEAD_DIM), v)
    if rope is not None:
        q = apply_rope(q, *rope)
        k = apply_rope(k, *rope)
    if ctx_k is not None:
        k = jnp.concatenate([ctx_k.reshape(b, -1, A_HEADS, 2, A_HEAD_DIM), k], axis=1)
        v = jnp.concatenate([ctx_v, v], axis=1)
    lam_f = lam.astype(jnp.float32)
    lam_val = jnp.exp(jnp.sum(lam_f[0] * lam_f[1])) - jnp.exp(jnp.sum(lam_f[2] * lam_f[3])) + lam_init
    o_a = diff_attention(q, k, v, lam_val)
    o_a = (rms_norm(o_a, subln_w) * (1 - lam_init)).reshape(b, n, A_WIDTH)
    o_b = spatial_gating(u, gv, sgu_w, sgu_b)
    o_c = fourier_mix(z)
    x = x + g1 * (jnp.concatenate([o_a, o_b, o_c], axis=-1) @ w_out)
    h2 = rms_norm(x, norm_ffn) * (1 + sc2) + sh2
    y = moe_ffn(h2.reshape(b * n, d), router_w, router_b, moe_w1, moe_b1, moe_w2, moe_b2)
    x = x + g2 * y.reshape(b, n, d)
    return x, kv_out


def setup_inputs(seed: int = 0) -> dict:
    key = jax.random.key(seed)
    ks = jax.random.split(key, 24)

    def nrm(k, shape, scale):
        return jax.random.normal(k, shape, jnp.float32) * scale

    return {
        'x_prompt': nrm(ks[0], (BATCH, SEQ, D_MODEL), 1.0),
        'x_sample': nrm(ks[1], (DEC_BATCH, DEC_SEQ, D_MODEL), 1.0),
        'cache_k': nrm(ks[2], (DEC_BATCH, DEPTH, PAST_LEN, A_HEADS, 2 * A_HEAD_DIM), 1.0),
        'cache_v': nrm(ks[3], (DEC_BATCH, DEPTH, PAST_LEN, A_HEADS, A_V_DIM), 1.0),
        'c': nrm(ks[4], (DEC_BATCH, D_MODEL), 1.0),
        'c_ctx': nrm(ks[5], (D_MODEL,), 1.0),
        'w_ada': nrm(ks[6], (DEPTH, D_MODEL, 6 * D_MODEL), 0.5 * D_MODEL ** -0.5),
        'b_ada': nrm(ks[7], (DEPTH, 6 * D_MODEL), 0.02),
        'norm_mix': 1.0 + nrm(ks[8], (DEPTH, D_MODEL), 0.05),
        'norm_ffn': 1.0 + nrm(ks[9], (DEPTH, D_MODEL), 0.05),
        'w_in': nrm(ks[10], (DEPTH, D_MODEL, IN_COLS), D_MODEL ** -0.5),
        'lam': nrm(ks[11], (DEPTH, 4, A_HEAD_DIM), 0.1),
        'subln_w': 1.0 + nrm(ks[12], (DEPTH, A_V_DIM), 0.05),
        'sgu_w': nrm(ks[13], (DEPTH, B_GROUPS, B_CHUNK, B_CHUNK), B_CHUNK ** -0.5),
        'sgu_b': 1.0 + nrm(ks[14], (DEPTH, B_GROUPS, B_CHUNK), 0.05),
        'w_out': nrm(ks[15], (DEPTH, D_MIX, D_MODEL), D_MIX ** -0.5),
        'router_w': nrm(ks[16], (DEPTH, D_MODEL, N_EXPERTS), D_MODEL ** -0.5),
        'router_b': nrm(ks[17], (DEPTH, N_EXPERTS), 0.01),
        'moe_w1': nrm(ks[18], (DEPTH, N_EXPERTS, D_MODEL, 2 * D_FF), D_MODEL ** -0.5),
        'moe_b1': nrm(ks[19], (DEPTH, N_EXPERTS, 2 * D_FF), 0.01),
        'moe_w2': nrm(ks[20], (DEPTH, N_EXPERTS, D_FF, D_MODEL), D_FF ** -0.5),
        'moe_b2': nrm(ks[21], (DEPTH, N_EXPERTS, D_MODEL), 0.01),
        'final_norm': 1.0 + nrm(ks[22], (D_MODEL,), 0.05),
    }


def reference(x_prompt, x_sample, cache_k, cache_v, c, c_ctx, w_ada, b_ada, norm_mix, norm_ffn, w_in, lam,
              subln_w, sgu_w, sgu_b, w_out, router_w, router_b, moe_w1, moe_b1, moe_w2, moe_b2, final_norm):
    rope = axial_rope(x_sample.shape[1])
    cond_ctx = c_ctx[None, :]
    xp, xs = x_prompt, x_sample
    ks, vs = [], []
    for l in range(DEPTH):
        lam_init = 0.8 - 0.6 * math.exp(-0.3 * l)
        weights = (w_ada[l], b_ada[l], norm_mix[l], norm_ffn[l], w_in[l], lam[l], subln_w[l], sgu_w[l],
                   sgu_b[l], w_out[l], router_w[l], router_b[l], moe_w1[l], moe_b1[l], moe_w2[l], moe_b2[l])
        xp, (k_l, v_l) = trunk_layer(xp, cond_ctx, None, None, None, lam_init, *weights)
        ks.append(k_l)
        vs.append(v_l)
        xs, _ = trunk_layer(xs, c, rope, cache_k[:, l], cache_v[:, l], lam_init, *weights)
    y_prompt = rms_norm(xp, final_norm)
    y_sample = rms_norm(xs, final_norm)
    new_k = jnp.stack(ks, axis=1)
    new_v = jnp.stack(vs, axis=1)
    return (y_prompt, y_sample, new_k, new_v)
```

```python
import functools
import math

import jax
import jax.numpy as jnp
from jax import lax
from jax.experimental import pallas as pl
from jax.experimental.pallas import tpu as pltpu

F32 = jnp.float32
BF16 = jnp.bfloat16
I32 = jnp.int32

D_MODEL = 2048
A_WIDTH = 1024
HEAD_COLS = 128
A_HEADS = 8
MAP_DIM = 64
ROPE_HALF = 32
B_WIDTH = 512
B_CHUNK = 128
B_GROUPS = 4
C_WIDTH = 512
C_GROUPS = 4
C_GROUP_DIM = 128
IN_COLS = 3 * A_WIDTH + 2 * B_WIDTH + C_WIDTH
N_EXPERTS = 32
TOP_K = 4
D_FF = 1024
SWIGLU_LIMIT = 7.0
SWIGLU_ALPHA = 1.702
GRID_W = 64
ROPE_THETA = 10000.0
EPS = 1e-6
LOG2E = 1.4426950408889634

LANES = 128
BF16_SUBLANES = 16
COND_ROWS = 16
VMEM_LIMIT = 56 * 1024 * 1024

TM = 256
TQ = 128
TR = 512
BM = 512
ADA_TN = 1024
FOURIER_TR = 512
SLAB = 512


def _cparams(n_axes):
    return pltpu.CompilerParams(dimension_semantics=("arbitrary",) * n_axes,
                                vmem_limit_bytes=VMEM_LIMIT)


def _dot(a, b):
    return jnp.dot(a, b, preferred_element_type=F32)


def _dot_nt(a, b):
    return lax.dot_general(a, b, (((1,), (1,)), ((), ())), preferred_element_type=F32)


def _split_bf16(x):
    hi = x.astype(BF16)
    lo = (x - hi.astype(F32)).astype(BF16)
    return hi, lo


def _ada_kernel(c_ref, w_ref, b_ref, o_ref):
    c = c_ref[...]
    c = c / (1.0 + jnp.exp(-c))
    c_hi, c_lo = _split_bf16(c)
    w_hi, w_lo = _split_bf16(w_ref[0])
    acc = _dot(c_hi, w_hi) + _dot(c_lo, w_hi) + _dot(c_hi, w_lo)
    o_ref[0] = acc + b_ref[0]


def _ada_call(cond, w_ada, b_ada):
    depth, d, n = w_ada.shape
    return pl.pallas_call(
        _ada_kernel,
        grid=(depth, n // ADA_TN),
        in_specs=[pl.BlockSpec((COND_ROWS, d), lambda l, j: (0, 0)),
                  pl.BlockSpec((1, d, ADA_TN), lambda l, j: (l, 0, j)),
                  pl.BlockSpec((1, 1, ADA_TN), lambda l, j: (l, 0, j))],
        out_specs=pl.BlockSpec((1, COND_ROWS, ADA_TN), lambda l, j: (l, 0, j)),
        out_shape=jax.ShapeDtypeStruct((depth, COND_ROWS, n), F32),
        compiler_params=_cparams(2),
        name="ada",
    )(cond, w_ada, b_ada.reshape(depth, 1, n))


def _proj_kernel(x_ref, mod_ref, g_ref, w_ref, cs_ref, cos_ref, sin_ref,
                 q0_ref, q1_ref, kvb_ref, kvf_ref, u_ref, gv_ref, zcs_ref,
                 *, n_rope_tiles, q_scale):
    i = pl.program_id(0)
    d = D_MODEL
    x = x_ref[...]
    ms = jnp.mean(x * x, axis=-1, keepdims=True)
    m = mod_ref[0]
    sh = m[:, 0:d]
    sc = m[:, d:2 * d]
    h = (x * lax.rsqrt(ms + EPS) * g_ref[...]) * (1.0 + sc) + sh
    hb = h.astype(BF16)

    use_rope = i < n_rope_tiles
    cos = jnp.where(use_rope, cos_ref[...], 1.0)
    sin = jnp.where(use_rope, sin_ref[...], 0.0)
    lane = lax.broadcasted_iota(I32, cos.shape, 1)
    first_half = (lane % MAP_DIM) < ROPE_HALF
    lo_map = lane < MAP_DIM

    def rope(t):
        swapped = jnp.where(first_half,
                            pltpu.roll(t, HEAD_COLS - ROPE_HALF, 1),
                            pltpu.roll(t, ROPE_HALF, 1))
        return t * cos + swapped * sin

    for hh in range(A_HEADS):
        c0 = hh * HEAD_COLS
        c1 = c0 + HEAD_COLS
        qh = rope(_dot(hb, w_ref[:, c0:c1])) * q_scale
        q0_ref[:, c0:c1] = jnp.where(lo_map, qh, 0.0).astype(BF16)
        q1_ref[:, c0:c1] = jnp.where(lo_map, 0.0, qh).astype(BF16)
        kh = _dot(hb, w_ref[:, A_WIDTH + c0:A_WIDTH + c1])
        kvf_ref[:, c0:c1] = kh
        kvb_ref[:, c0:c1] = rope(kh).astype(BF16)
    v = _dot(hb, w_ref[:, 2 * A_WIDTH:3 * A_WIDTH])
    kvf_ref[:, A_WIDTH:] = v
    kvb_ref[:, A_WIDTH:] = v.astype(BF16)
    o = 3 * A_WIDTH
    u_ref[...] = _dot(hb, w_ref[:, o:o + B_WIDTH]).astype(BF16)
    o += B_WIDTH
    gv_ref[...] = _dot(hb, w_ref[:, o:o + B_WIDTH]).astype(BF16)
    o += B_WIDTH
    zb = _dot(hb, w_ref[:, o:o + C_WIDTH]).astype(BF16)
    for g in range(C_GROUPS):
        c0 = g * C_GROUP_DIM
        c1 = c0 + C_GROUP_DIM
        zz = _dot(zb[:, c0:c1], cs_ref[...])
        zcs_ref[:, c0:c1] = zz[:, :C_GROUP_DIM].astype(BF16)
        zcs_ref[:, C_WIDTH + c0:C_WIDTH + c1] = zz[:, C_GROUP_DIM:].astype(BF16)


def _proj_call(x, mod_l, norm_g, w_in_b, cs_chan, cos_t, sin_t, geo):
    t_tok, d = x.shape
    n_tiles = t_tok // TM
    ns_tiles = geo["t_sample"] // TM
    rope_blocks = geo["sample_seq"] // TM
    n_prompt = t_tok - geo["t_sample"]
    sample_seq = geo["sample_seq"]
    n_sb = geo["n_sample_b"]

    def mod_row(i):
        return (jnp.minimum((i * TM) // sample_seq, n_sb), 0, 0)

    kern = functools.partial(_proj_kernel, n_rope_tiles=ns_tiles,
                             q_scale=float(MAP_DIM ** -0.5 * LOG2E))
    row_blk = lambda i: (i, 0)
    const = lambda i: (0, 0)
    return pl.pallas_call(
        kern,
        grid=(n_tiles,),
        in_specs=[pl.BlockSpec((TM, d), row_blk),
                  pl.BlockSpec((1, 1, 6 * d), mod_row),
                  pl.BlockSpec((1, d), const),
                  pl.BlockSpec((d, IN_COLS), const, pipeline_mode=pl.Buffered(1)),
                  pl.BlockSpec((C_GROUP_DIM, 2 * C_GROUP_DIM), const),
                  pl.BlockSpec((TM, HEAD_COLS), lambda i: (i % rope_blocks, 0)),
                  pl.BlockSpec((TM, HEAD_COLS), lambda i: (i % rope_blocks, 0))],
        out_specs=[pl.BlockSpec((TM, A_WIDTH), row_blk),
                   pl.BlockSpec((TM, A_WIDTH), row_blk),
                   pl.BlockSpec((TM, 2 * A_WIDTH), row_blk),
                   pl.BlockSpec((TM, 2 * A_WIDTH), lambda i: (jnp.maximum(i - ns_tiles, 0), 0)),
                   pl.BlockSpec((TM, B_WIDTH), row_blk),
                   pl.BlockSpec((TM, B_WIDTH), row_blk),
                   pl.BlockSpec((TM, 2 * C_WIDTH), row_blk)],
        out_shape=[jax.ShapeDtypeStruct((t_tok, A_WIDTH), BF16),
                   jax.ShapeDtypeStruct((t_tok, A_WIDTH), BF16),
                   jax.ShapeDtypeStruct((t_tok, 2 * A_WIDTH), BF16),
                   jax.ShapeDtypeStruct((n_prompt, 2 * A_WIDTH), F32),
                   jax.ShapeDtypeStruct((t_tok, B_WIDTH), BF16),
                   jax.ShapeDtypeStruct((t_tok, B_WIDTH), BF16),
                   jax.ShapeDtypeStruct((t_tok, 2 * C_WIDTH), BF16)],
        compiler_params=_cparams(1),
        name="proj",
    )(x, mod_l, norm_g, w_in_b, cs_chan, cos_t, sin_t)


def _attn_kernel(*refs, has_cache, lam_init, n_seq, past, tq):
    if has_cache:
        (q0_ref, q1_ref, k_ref, v_ref, ck_ref, cv_ref, lam_ref, sw_ref, o_ref,
         s_scr, a_scr, kc_scr, vc_scr) = refs
        kc_scr[...] = ck_ref[0, 0].astype(BF16)
        vc_scr[...] = cv_ref[0, 0].astype(BF16)
    else:
        q0_ref, q1_ref, k_ref, v_ref, lam_ref, sw_ref, o_ref, s_scr, a_scr = refs
    lam = lam_ref[...]
    lam_val = (jnp.exp(jnp.sum(lam[0:1] * lam[1:2], axis=-1, keepdims=True))
               - jnp.exp(jnp.sum(lam[2:3] * lam[3:4], axis=-1, keepdims=True)) + lam_init)
    sw = sw_ref[...] * (1.0 - lam_init)

    def body(j, carry):
        r0 = pl.multiple_of(j * tq, tq)
        qq = jnp.concatenate([q0_ref[pl.ds(r0, tq), :], q1_ref[pl.ds(r0, tq), :]], axis=0)
        if has_cache:
            s_scr[:, :past] = _dot_nt(qq, kc_scr[...])
        s_scr[:, past:] = _dot_nt(qq, k_ref[...])
        m = jnp.max(s_scr[...], axis=-1, keepdims=True)
        p = jnp.exp2(s_scr[...] - m)
        s_scr[...] = p
        r = 1.0 / jnp.sum(p, axis=-1, keepdims=True)
        w0 = r[:tq]
        w1 = r[tq:] * lam_val
        a_scr[...] = (s_scr[:tq, :] * w0 - s_scr[tq:, :] * w1).astype(BF16)
        o = _dot(a_scr[:, past:], v_ref[...])
        if has_cache:
            o = o + _dot(a_scr[:, :past], vc_scr[...])
        o = o * lax.rsqrt(jnp.mean(o * o, axis=-1, keepdims=True) + EPS) * sw
        o_ref[pl.ds(r0, tq), :] = o.astype(BF16)
        return carry

    lax.fori_loop(0, n_seq // tq, body, 0)


def _attn_call(q0, q1, kvb, lam_l, subln_l, lam_init, n_batch, n_seq, row_off, cache=None):
    blk_off = row_off // n_seq
    tq = min(TQ, n_seq)
    has_cache = cache is not None
    q_spec = pl.BlockSpec((n_seq, HEAD_COLS), lambda b, h: (blk_off + b, h))
    in_specs = [q_spec, q_spec,
                pl.BlockSpec((n_seq, HEAD_COLS), lambda b, h: (blk_off + b, h)),
                pl.BlockSpec((n_seq, HEAD_COLS), lambda b, h: (blk_off + b, A_HEADS + h))]
    args = [q0, q1, kvb, kvb]
    past = 0
    if has_cache:
        ck, cv, layer = cache
        past = ck.shape[2]
        c_spec = pl.BlockSpec((1, 1, past, HEAD_COLS), lambda b, h: (b, layer, 0, h))
        in_specs += [c_spec, c_spec]
        args += [ck, cv]
    scratch = [pltpu.VMEM((2 * tq, past + n_seq), F32), pltpu.VMEM((tq, past + n_seq), BF16)]
    if has_cache:
        scratch += [pltpu.VMEM((past, HEAD_COLS), BF16), pltpu.VMEM((past, HEAD_COLS), BF16)]
    in_specs += [pl.BlockSpec((4, MAP_DIM), lambda b, h: (0, 0)),
                 pl.BlockSpec((1, HEAD_COLS), lambda b, h: (0, 0))]
    args += [lam_l, subln_l]
    kern = functools.partial(_attn_kernel, has_cache=has_cache, lam_init=float(lam_init),
                             n_seq=n_seq, past=past, tq=tq)
    return pl.pallas_call(
        kern,
        grid=(n_batch, A_HEADS),
        in_specs=in_specs,
        out_specs=pl.BlockSpec((n_seq, HEAD_COLS), lambda b, h: (b, h)),
        out_shape=jax.ShapeDtypeStruct((n_batch * n_seq, A_WIDTH), BF16),
        scratch_shapes=scratch,
        compiler_params=_cparams(2),
        name="attn_cache" if has_cache else "attn",
    )(*args)


def _fourier_kernel(c_ref, s_ref, z_ref, o_ref, *, scale):
    zc = z_ref[:, :C_WIDTH]
    zs = z_ref[:, C_WIDTH:]
    y = _dot(c_ref[...], zc) - _dot(s_ref[...], zs)
    o_ref[...] = (y * scale).astype(BF16)


def _fourier_call(zcs, cos_n, sin_n, n_batch, n_seq, row_off):
    tr = min(FOURIER_TR, n_seq)
    n_f = n_seq // tr
    blk_off = row_off // n_seq
    kern = functools.partial(_fourier_kernel, scale=float((n_seq * C_GROUP_DIM) ** -0.5))
    return pl.pallas_call(
        kern,
        grid=(n_batch, n_f),
        in_specs=[pl.BlockSpec((tr, n_seq), lambda b, f: (f, 0)),
                  pl.BlockSpec((tr, n_seq), lambda b, f: (f, 0)),
                  pl.BlockSpec((n_seq, 2 * C_WIDTH), lambda b, f: (blk_off + b, 0))],
        out_specs=pl.BlockSpec((tr, C_WIDTH), lambda b, f: (b * n_f + f, 0)),
        out_shape=jax.ShapeDtypeStruct((n_batch * n_seq, C_WIDTH), BF16),
        compiler_params=_cparams(2),
        name="fourier",
    )(cos_n, sin_n, zcs)


def _out_kernel(oa_s_ref, oa_p_ref, oc_s_ref, oc_p_ref, u_ref, gv_ref, x_ref, mod_ref,
                sgw_ref, sgb_ref, w_ref, g_ref, rw_hi_ref, rw_lo_ref, rb_ref,
                xo_ref, h2_ref, lg_ref, mix_scr, *, ns_tiles):
    i = pl.program_id(0)
    d = D_MODEL

    @pl.when(i < ns_tiles)
    def _():
        mix_scr[:, :A_WIDTH] = oa_s_ref[...]
        mix_scr[:, A_WIDTH + B_WIDTH:] = oc_s_ref[...]

    @pl.when(i >= ns_tiles)
    def _():
        mix_scr[:, :A_WIDTH] = oa_p_ref[...]
        mix_scr[:, A_WIDTH + B_WIDTH:] = oc_p_ref[...]

    for c in range(TM // B_CHUNK):
        r0 = c * B_CHUNK
        for g in range(B_GROUPS):
            c0 = g * (B_WIDTH // B_GROUPS)
            c1 = c0 + B_WIDTH // B_GROUPS
            gate = _dot(sgw_ref[g], gv_ref[r0:r0 + B_CHUNK, c0:c1]) + sgb_ref[:, g:g + 1]
            ob = u_ref[r0:r0 + B_CHUNK, c0:c1].astype(F32) * gate
            mix_scr[r0:r0 + B_CHUNK, A_WIDTH + c0:A_WIDTH + c1] = ob.astype(BF16)

    m = mod_ref[0]
    g1 = m[:, 2 * d:3 * d]
    sh2 = m[:, 3 * d:4 * d]
    sc2 = m[:, 4 * d:5 * d]
    y = _dot(mix_scr[...], w_ref[...])
    xn = x_ref[...] + g1 * y
    xo_ref[...] = xn
    ms = jnp.mean(xn * xn, axis=-1, keepdims=True)
    h2 = (xn * lax.rsqrt(ms + EPS) * g_ref[...]) * (1.0 + sc2) + sh2
    h_hi, h_lo = _split_bf16(h2)
    h2_ref[...] = h_hi
    logits = (_dot_nt(rw_hi_ref[...], h_hi) + _dot_nt(rw_hi_ref[...], h_lo)
              + _dot_nt(rw_lo_ref[...], h_hi))
    lg_ref[...] = logits + rb_ref[...]


def _out_call(oa_s, oa_p, oc_s, oc_p, u, gv, x, mod_l, sgw_b, sgb_t, w_out_b, norm_g,
              rw_hi, rw_lo, rb_col, geo):
    t_tok, d = x.shape
    n_tiles = t_tok // TM
    ns_tiles = geo["t_sample"] // TM
    sample_seq = geo["sample_seq"]
    n_sb = geo["n_sample_b"]

    def mod_row(i):
        return (jnp.minimum((i * TM) // sample_seq, n_sb), 0, 0)

    row_blk = lambda i: (i, 0)
    s_blk = lambda i: (jnp.minimum(i, ns_tiles - 1), 0)
    p_blk = lambda i: (jnp.maximum(i - ns_tiles, 0), 0)
    const2 = lambda i: (0, 0)
    kern = functools.partial(_out_kernel, ns_tiles=ns_tiles)
    return pl.pallas_call(
        kern,
        grid=(n_tiles,),
        in_specs=[pl.BlockSpec((TM, A_WIDTH), s_blk),
                  pl.BlockSpec((TM, A_WIDTH), p_blk),
                  pl.BlockSpec((TM, C_WIDTH), s_blk),
                  pl.BlockSpec((TM, C_WIDTH), p_blk),
                  pl.BlockSpec((TM, B_WIDTH), row_blk),
                  pl.BlockSpec((TM, B_WIDTH), row_blk),
                  pl.BlockSpec((TM, d), row_blk),
                  pl.BlockSpec((1, 1, 6 * d), mod_row),
                  pl.BlockSpec((B_GROUPS, B_CHUNK, B_CHUNK), lambda i: (0, 0, 0)),
                  pl.BlockSpec((B_CHUNK, B_GROUPS), const2),
                  pl.BlockSpec((d, d), const2, pipeline_mode=pl.Buffered(1)),
                  pl.BlockSpec((1, d), const2),
                  pl.BlockSpec((N_EXPERTS, d), const2),
                  pl.BlockSpec((N_EXPERTS, d), const2),
                  pl.BlockSpec((N_EXPERTS, 1), const2)],
        out_specs=[pl.BlockSpec((TM, d), row_blk),
                   pl.BlockSpec((TM, d), row_blk),
                   pl.BlockSpec((N_EXPERTS, TM), lambda i: (0, i))],
        out_shape=[jax.ShapeDtypeStruct((t_tok, d), F32),
                   jax.ShapeDtypeStruct((t_tok, d), BF16),
                   jax.ShapeDtypeStruct((N_EXPERTS, t_tok), F32)],
        scratch_shapes=[pltpu.VMEM((TM, d), BF16)],
        compiler_params=_cparams(1),
        name="out",
    )(oa_s, oa_p, oc_s, oc_p, u, gv, x, mod_l, sgw_b, sgb_t, w_out_b, norm_g,
      rw_hi, rw_lo, rb_col)


def _route_kernel(lg_ref, idx_ref, gate_ref, rank_ref, cnt_ref):
    lg = lg_ref[...]
    n_e, tr = lg.shape
    eidx = lax.broadcasted_iota(I32, lg.shape, 0).astype(F32)
    work = lg
    vals = []
    sels = []
    for _ in range(TOP_K):
        mk = jnp.max(work, axis=0, keepdims=True)
        ik = jnp.min(jnp.where(work == mk, eidx, float(n_e)), axis=0, keepdims=True)
        hit = eidx == ik
        work = jnp.where(hit, -jnp.inf, work)
        vals.append(mk)
        sels.append(hit)
    es = [jnp.exp(v - vals[0]) for v in vals]
    inv = 1.0 / (es[0] + es[1] + es[2] + es[3])
    onehot = jnp.where(sels[0] | sels[1] | sels[2] | sels[3], 1.0, 0.0)
    before = (lax.broadcasted_iota(I32, (tr, tr), 0) < lax.broadcasted_iota(I32, (tr, tr), 1))
    upper = jnp.where(before, 1.0, 0.0).astype(BF16)
    prefix = _dot(onehot.astype(BF16), upper)
    for k in range(TOP_K):
        idx_ref[k:k + 1, :] = jnp.sum(jnp.where(sels[k], eidx, 0.0), axis=0, keepdims=True).astype(I32)
        rank_ref[k:k + 1, :] = jnp.sum(jnp.where(sels[k], prefix, 0.0), axis=0, keepdims=True).astype(I32)
        gate_ref[k:k + 1, :] = es[k] * inv
    cnt_ref[0] = jnp.sum(onehot, axis=1, keepdims=True).astype(I32)


def _route_call(logits_t):
    n_e, t_tok = logits_t.shape
    n_tiles = t_tok // TR
    tok_blk = lambda i: (0, i)
    return pl.pallas_call(
        _route_kernel,
        grid=(n_tiles,),
        in_specs=[pl.BlockSpec((n_e, TR), tok_blk)],
        out_specs=[pl.BlockSpec((TOP_K, TR), tok_blk),
                   pl.BlockSpec((TOP_K, TR), tok_blk),
                   pl.BlockSpec((TOP_K, TR), tok_blk),
                   pl.BlockSpec((1, n_e, 1), lambda i: (i, 0, 0))],
        out_shape=[jax.ShapeDtypeStruct((TOP_K, t_tok), I32),
                   jax.ShapeDtypeStruct((TOP_K, t_tok), F32),
                   jax.ShapeDtypeStruct((TOP_K, t_tok), I32),
                   jax.ShapeDtypeStruct((n_tiles, n_e, 1), I32)],
        compiler_params=_cparams(1),
        name="route",
    )(logits_t)


def _rows_local():
    return TOP_K * TR + N_EXPERTS * BF16_SUBLANES


def _plan(counts, t_tok):
    n_tiles = t_tok // TR
    g = BF16_SUBLANES
    rows_loc = _rows_local()
    n_chunk_max = rows_loc // g
    counts = counts.reshape(n_tiles, N_EXPERTS)
    padded = (counts + g - 1) // g * g
    loc_end = jnp.cumsum(padded, axis=1)
    loc_start = loc_end - padded
    tot = jnp.sum(padded, axis=0)
    region = (tot + BM - 1) // BM * BM
    reg_end = jnp.cumsum(region)
    reg_start = reg_end - region
    seg_off = reg_start[None, :] + jnp.cumsum(padded, axis=0) - padded
    n_chunks = loc_end[:, -1] // g
    chunk_row = jnp.arange(n_chunk_max, dtype=I32) * g
    chunk_e = jnp.sum((loc_end[:, None, :] <= chunk_row[None, :, None]).astype(I32), axis=-1)
    chunk_e = jnp.minimum(chunk_e, N_EXPERTS - 1)
    take = lambda a: jnp.take_along_axis(a, chunk_e, axis=1)
    chunk_dst = take(seg_off) + chunk_row[None, :] - take(loc_start)
    n_rows_max = (TOP_K * t_tok + n_tiles * N_EXPERTS * (g - 1) + N_EXPERTS * (BM - g))
    n_blocks_max = -(-n_rows_max // BM)
    n_used = reg_end[-1] // BM
    blk_row = jnp.arange(n_blocks_max, dtype=I32) * BM
    blk_e = jnp.sum((reg_end[None, :] <= blk_row[:, None]).astype(I32), axis=-1)
    blk_e = jnp.minimum(blk_e, N_EXPERTS - 1)
    pad_start = reg_start + tot
    pad_chunks = (region - tot) // g
    return dict(loc_start=loc_start.reshape(-1).astype(I32),
                n_chunks=n_chunks.astype(I32),
                chunk_dst=chunk_dst.reshape(-1).astype(I32),
                blk_e=blk_e.astype(I32),
                n_used=n_used.reshape(1).astype(I32),
                pad_start=pad_start.astype(I32),
                pad_chunks=pad_chunks.astype(I32),
                n_blocks_max=n_blocks_max)


def _dispatch_kernel(ls_ref, nch_ref, cdst_ref, pst_ref, pch_ref, nused_ref,
                     h_ref, idx_ref, rank_ref, xs_hbm, loc_scr, zero_scr, sem, zsem,
                     *, n_blocks_max):
    i = pl.program_id(0)
    n_tiles = pl.num_programs(0)
    g = BF16_SUBLANES
    rows_loc = loc_scr.shape[0]
    n_chunk_max = rows_loc // g
    idx = idx_ref[...]
    lrow = rank_ref[...]
    for e in range(N_EXPERTS):
        lrow = lrow + jnp.where(idx == e, ls_ref[i * N_EXPERTS + e], 0)

    def sort_slab(s, carry):
        r0 = pl.multiple_of(s * SLAB, SLAB)
        riota = lax.broadcasted_iota(I32, (SLAB, TR), 0) + r0
        sel = jnp.where(riota == lrow[0:1, :], 1.0, 0.0)
        for k in range(1, TOP_K):
            sel = jnp.where(riota == lrow[k:k + 1, :], 1.0, sel)
        loc_scr[pl.ds(r0, SLAB), :] = _dot(sel.astype(BF16), h_ref[...]).astype(BF16)
        return carry

    lax.fori_loop(0, rows_loc // SLAB, sort_slab, 0)

    def seg_copy(c):
        src = pl.multiple_of(c * g, g)
        dst = pl.multiple_of(cdst_ref[i * n_chunk_max + c], g)
        return pltpu.make_async_copy(loc_scr.at[pl.ds(src, g), :], xs_hbm.at[pl.ds(dst, g), :], sem)

    n_ch = nch_ref[i]

    def start(c, carry):
        seg_copy(c).start()
        return carry

    def wait(c, carry):
        seg_copy(c).wait()
        return carry

    lax.fori_loop(0, n_ch, start, 0)
    lax.fori_loop(0, n_ch, wait, 0)

    @pl.when(i == n_tiles - 1)
    def _():
        zero_scr[...] = jnp.zeros_like(zero_scr)

        def pad_copy(row):
            row = pl.multiple_of(row, g)
            return pltpu.make_async_copy(zero_scr.at[pl.ds(0, g), :], xs_hbm.at[pl.ds(row, g), :], zsem)

        def blk_copy(b):
            row = pl.multiple_of(b * BM, BM)
            return pltpu.make_async_copy(zero_scr, xs_hbm.at[pl.ds(row, BM), :], zsem)

        for e in range(N_EXPERTS):
            def pstart(j, carry, e=e):
                pad_copy(pst_ref[e] + j * g).start()
                return carry

            def pwait(j, carry, e=e):
                pad_copy(pst_ref[e] + j * g).wait()
                return carry

            lax.fori_loop(0, pch_ref[e], pstart, 0)
            lax.fori_loop(0, pch_ref[e], pwait, 0)

        def bstart(b, carry):
            blk_copy(b).start()
            return carry

        def bwait(b, carry):
            blk_copy(b).wait()
            return carry

        lax.fori_loop(nused_ref[0], n_blocks_max, bstart, 0)
        lax.fori_loop(nused_ref[0], n_blocks_max, bwait, 0)


def _dispatch_call(plan, h2, idx_t, rank_t):
    t_tok, d = h2.shape
    n_tiles = t_tok // TR
    rows_loc = _rows_local()
    n_blocks_max = plan["n_blocks_max"]
    kern = functools.partial(_dispatch_kernel, n_blocks_max=n_blocks_max)
    grid_spec = pltpu.PrefetchScalarGridSpec(
        num_scalar_prefetch=6,
        grid=(n_tiles,),
        in_specs=[pl.BlockSpec((TR, d), lambda i, *_: (i, 0)),
                  pl.BlockSpec((TOP_K, TR), lambda i, *_: (0, i)),
                  pl.BlockSpec((TOP_K, TR), lambda i, *_: (0, i))],
        out_specs=pl.BlockSpec(memory_space=pl.ANY),
        scratch_shapes=[pltpu.VMEM((rows_loc, d), BF16),
                        pltpu.VMEM((BM, d), BF16),
                        pltpu.SemaphoreType.DMA(()),
                        pltpu.SemaphoreType.DMA(())])
    return pl.pallas_call(
        kern,
        grid_spec=grid_spec,
        out_shape=jax.ShapeDtypeStruct((n_blocks_max * BM, d), BF16),
        compiler_params=_cparams(1),
        name="dispatch",
    )(plan["loc_start"], plan["n_chunks"], plan["chunk_dst"], plan["pad_start"],
      plan["pad_chunks"], plan["n_used"], h2, idx_t, rank_t)


def _expert_kernel(be_ref, nused_ref, x_ref, w1_ref, b1_ref, w2_ref, b2_ref, y_ref):
    b = pl.program_id(0)

    @pl.when(b < nused_ref[0])
    def _():
        gu = _dot(x_ref[...], w1_ref[0]) + b1_ref[0]
        gate = jnp.minimum(gu[:, :D_FF], SWIGLU_LIMIT)
        up = jnp.clip(gu[:, D_FF:], -SWIGLU_LIMIT, SWIGLU_LIMIT)
        act = (up + 1.0) * gate * (1.0 / (1.0 + jnp.exp(-SWIGLU_ALPHA * gate)))
        y = _dot(act.astype(BF16), w2_ref[0]) + b2_ref[0]
        y_ref[...] = y.astype(BF16)

    @pl.when(b >= nused_ref[0])
    def _():
        y_ref[...] = jnp.zeros_like(y_ref)


def _expert_call(plan, xs, w1_b, b1, w2_b, b2):
    n_rows, d = xs.shape
    n_blocks = n_rows // BM
    n_e = w1_b.shape[0]
    grid_spec = pltpu.PrefetchScalarGridSpec(
        num_scalar_prefetch=2,
        grid=(n_blocks,),
        in_specs=[pl.BlockSpec((BM, d), lambda b, be, nu: (jnp.minimum(b, nu[0] - 1), 0)),
                  pl.BlockSpec((1, d, 2 * D_FF), lambda b, be, nu: (be[b], 0, 0)),
                  pl.BlockSpec((1, 1, 2 * D_FF), lambda b, be, nu: (be[b], 0, 0)),
                  pl.BlockSpec((1, D_FF, d), lambda b, be, nu: (be[b], 0, 0)),
                  pl.BlockSpec((1, 1, d), lambda b, be, nu: (be[b], 0, 0))],
        out_specs=pl.BlockSpec((BM, d), lambda b, be, nu: (b, 0)))
    return pl.pallas_call(
        _expert_kernel,
        grid_spec=grid_spec,
        out_shape=jax.ShapeDtypeStruct((n_rows, d), BF16),
        compiler_params=_cparams(1),
        name="experts",
    )(plan["blk_e"], plan["n_used"], xs, w1_b, b1.reshape(n_e, 1, 2 * D_FF), w2_b,
      b2.reshape(n_e, 1, d))


def _combine_kernel(ls_ref, nch_ref, cdst_ref, x_ref, mod_ref, idx_ref, rank_ref, gate_ref,
                    fin_ref, ys_hbm, o_ref, loc_scr, acc_scr, sem, *, final):
    i = pl.program_id(0)
    g = BF16_SUBLANES
    d = D_MODEL
    rows_loc = loc_scr.shape[0]
    n_chunk_max = rows_loc // g

    @pl.when(i == 0)
    def _():
        loc_scr[...] = jnp.zeros_like(loc_scr)

    def seg_copy(c):
        dst = pl.multiple_of(c * g, g)
        src = pl.multiple_of(cdst_ref[i * n_chunk_max + c], g)
        return pltpu.make_async_copy(ys_hbm.at[pl.ds(src, g), :], loc_scr.at[pl.ds(dst, g), :], sem)

    n_ch = nch_ref[i]

    def start(c, carry):
        seg_copy(c).start()
        return carry

    def wait(c, carry):
        seg_copy(c).wait()
        return carry

    lax.fori_loop(0, n_ch, start, 0)

    idx = idx_ref[...]
    lrow = rank_ref[...]
    for e in range(N_EXPERTS):
        lrow = lrow + jnp.where(idx == e, ls_ref[i * N_EXPERTS + e], 0)
    gates = gate_ref[...]

    lax.fori_loop(0, n_ch, wait, 0)

    def gather_slab(s, carry):
        c0 = pl.multiple_of(s * SLAB, SLAB)
        ciota = lax.broadcasted_iota(I32, (TR, SLAB), 1) + c0
        wsel = jnp.where(ciota == lrow[:, 0:1], gates[:, 0:1], 0.0)
        for k in range(1, TOP_K):
            wsel = jnp.where(ciota == lrow[:, k:k + 1], gates[:, k:k + 1], wsel)
        part = _dot(wsel.astype(BF16), loc_scr[pl.ds(c0, SLAB), :])

        @pl.when(s == 0)
        def _():
            acc_scr[...] = part

        @pl.when(s > 0)
        def _():
            acc_scr[...] += part

        return carry

    lax.fori_loop(0, rows_loc // SLAB, gather_slab, 0)
    g2 = mod_ref[0][:, 5 * d:6 * d]
    xn = x_ref[...] + g2 * acc_scr[...]
    if final:
        ms = jnp.mean(xn * xn, axis=-1, keepdims=True)
        xn = xn * lax.rsqrt(ms + EPS) * fin_ref[...]
    o_ref[...] = xn


def _combine_call(plan, x, mod_l, idx_c, rank_c, gate_c, final_g, ys, geo, final):
    t_tok, d = x.shape
    n_tiles = t_tok // TR
    rows_loc = _rows_local()
    sample_seq = geo["sample_seq"]
    n_sb = geo["n_sample_b"]

    def mod_row(i, *_):
        return (jnp.minimum((i * TR) // sample_seq, n_sb), 0, 0)

    row_blk = lambda i, *_: (i, 0)
    kern = functools.partial(_combine_kernel, final=final)
    grid_spec = pltpu.PrefetchScalarGridSpec(
        num_scalar_prefetch=3,
        grid=(n_tiles,),
        in_specs=[pl.BlockSpec((TR, d), row_blk),
                  pl.BlockSpec((1, 1, 6 * d), mod_row),
                  pl.BlockSpec((TR, TOP_K), row_blk),
                  pl.BlockSpec((TR, TOP_K), row_blk),
                  pl.BlockSpec((TR, TOP_K), row_blk),
                  pl.BlockSpec((1, d), lambda i, *_: (0, 0)),
                  pl.BlockSpec(memory_space=pl.ANY)],
        out_specs=pl.BlockSpec((TR, d), row_blk),
        scratch_shapes=[pltpu.VMEM((rows_loc, d), BF16),
                        pltpu.VMEM((TR, d), F32),
                        pltpu.SemaphoreType.DMA(())])
    return pl.pallas_call(
        kern,
        grid_spec=grid_spec,
        out_shape=jax.ShapeDtypeStruct((t_tok, d), F32),
        compiler_params=_cparams(1),
        name="combine_final" if final else "combine",
    )(plan["loc_start"], plan["n_chunks"], plan["chunk_dst"], x, mod_l, idx_c, rank_c, gate_c,
      final_g, ys)


def _rope_tables(n_tok):
    rows = n_tok // GRID_W
    row = jnp.repeat(jnp.arange(rows, dtype=F32), GRID_W)
    col = jnp.tile(jnp.arange(GRID_W, dtype=F32), rows)
    pairs = ROPE_HALF // 2
    freqs = ROPE_THETA ** (-jnp.arange(pairs, dtype=F32) / pairs)
    ang = jnp.concatenate([row[:, None] * freqs, col[:, None] * freqs], axis=-1)
    cos, sin = jnp.cos(ang), jnp.sin(ang)
    cos_t = jnp.concatenate([cos, cos, cos, cos], axis=-1)
    sin_t = jnp.concatenate([-sin, sin, -sin, sin], axis=-1)
    return cos_t, sin_t


def _dft_tables(n):
    k = jnp.arange(n, dtype=I32)
    ang = ((k[:, None] * k[None, :]) % n).astype(F32) * (2.0 * math.pi / n)
    return jnp.cos(ang), jnp.sin(ang)


def _forward(x_prompt, x_sample, cache_k, cache_v, c, c_ctx, w_ada, b_ada, norm_mix, norm_ffn,
             w_in, lam, subln_w, sgu_w, sgu_b, w_out, router_w, router_b, moe_w1, moe_b1,
             moe_w2, moe_b2, final_norm):
    n_pb, p_seq, d = x_prompt.shape
    n_sb, s_seq, _ = x_sample.shape
    depth = w_ada.shape[0]
    past = cache_k.shape[2]
    t_sample = n_sb * s_seq
    t_prompt = n_pb * p_seq
    t_tok = t_sample + t_prompt
    geo = dict(t_sample=t_sample, sample_seq=s_seq, n_sample_b=n_sb)

    x = jnp.concatenate([x_sample.reshape(t_sample, d), x_prompt.reshape(t_prompt, d)], axis=0)
    cond = jnp.zeros((COND_ROWS, d), F32).at[:n_sb].set(c).at[n_sb].set(c_ctx)
    mod = _ada_call(cond, w_ada, b_ada)

    cos_t, sin_t = _rope_tables(s_seq)
    cc, sc = _dft_tables(C_GROUP_DIM)
    cs_chan = jnp.concatenate([cc, sc], axis=1).astype(BF16)
    cos_s, sin_s = [a.astype(BF16) for a in _dft_tables(s_seq)]
    cos_p, sin_p = [a.astype(BF16) for a in _dft_tables(p_seq)]
    ck = cache_k.reshape(n_sb, depth, past, A_WIDTH)
    cv = cache_v.reshape(n_sb, depth, past, A_WIDTH)

    new_k, new_v = [], []
    for l in range(depth):
        lam_init = 0.8 - 0.6 * math.exp(-0.3 * l)
        mod_l = mod[l].reshape(COND_ROWS, 1, 6 * d)
        q0, q1, kvb, kvf, u, gv, zcs = _proj_call(
            x, mod_l, norm_mix[l].reshape(1, d), w_in[l].astype(BF16), cs_chan, cos_t, sin_t, geo)
        new_k.append(kvf[:, :A_WIDTH].reshape(n_pb, p_seq, A_HEADS, HEAD_COLS))
        new_v.append(kvf[:, A_WIDTH:].reshape(n_pb, p_seq, A_HEADS, HEAD_COLS))
        subln_l = subln_w[l].reshape(1, HEAD_COLS)
        oa_s = _attn_call(q0, q1, kvb, lam[l], subln_l, lam_init, n_sb, s_seq, 0, cache=(ck, cv, l))
        oa_p = _attn_call(q0, q1, kvb, lam[l], subln_l, lam_init, n_pb, p_seq, t_sample)
        oc_s = _fourier_call(zcs, cos_s, sin_s, n_sb, s_seq, 0)
        oc_p = _fourier_call(zcs, cos_p, sin_p, n_pb, p_seq, t_sample)
        rw_hi, rw_lo = _split_bf16(router_w[l].T)
        x, h2, logits_t = _out_call(
            oa_s, oa_p, oc_s, oc_p, u, gv, x, mod_l, sgu_w[l].astype(BF16), sgu_b[l].T,
            w_out[l].astype(BF16), norm_ffn[l].reshape(1, d), rw_hi, rw_lo,
            router_b[l].reshape(N_EXPERTS, 1), geo)
        idx_t, gate_t, rank_t, counts = _route_call(logits_t)
        plan = _plan(counts, t_tok)
        xs = _dispatch_call(plan, h2, idx_t, rank_t)
        ys = _expert_call(plan, xs, moe_w1[l].astype(BF16), moe_b1[l], moe_w2[l].astype(BF16),
                          moe_b2[l])
        x = _combine_call(plan, x, mod_l, idx_t.T, rank_t.T, gate_t.T, final_norm.reshape(1, d),
                          ys, geo, final=(l == depth - 1))

    y_sample = x[:t_sample].reshape(n_sb, s_seq, d)
    y_prompt = x[t_sample:].reshape(n_pb, p_seq, d)
    return y_prompt, y_sample, jnp.stack(new_k, axis=1), jnp.stack(new_v, axis=1)


def kernel(x_prompt, x_sample, cache_k, cache_v, c, c_ctx, w_ada, b_ada, norm_mix, norm_ffn, w_in,
           lam, subln_w, sgu_w, sgu_b, w_out, router_w, router_b, moe_w1, moe_b1, moe_w2, moe_b2,
           final_norm):
    return _forward(x_prompt, x_sample, cache_k, cache_v, c, c_ctx, w_ada, b_ada, norm_mix,
                    norm_ffn, w_in, lam, subln_w, sgu_w, sgu_b, w_out, router_w, router_b,
                    moe_w1, moe_b1, moe_w2, moe_b2, final_norm)
```

```python
import functools
import math

import jax
import jax.numpy as jnp
from jax import lax
from jax.experimental import pallas as pl
from jax.experimental.pallas import tpu as pltpu

F32 = jnp.float32
BF16 = jnp.bfloat16
I32 = jnp.int32

D_MODEL = 2048
A_WIDTH = 1024
HEAD_COLS = 128
A_HEADS = 8
MAP_DIM = 64
ROPE_HALF = 32
B_WIDTH = 512
B_CHUNK = 128
B_GROUPS = 4
C_WIDTH = 512
C_GROUPS = 4
C_GROUP_DIM = 128
IN_COLS = 3 * A_WIDTH + 2 * B_WIDTH + C_WIDTH
N_EXPERTS = 32
TOP_K = 4
D_FF = 1024
SWIGLU_LIMIT = 7.0
SWIGLU_ALPHA = 1.702
GRID_W = 64
ROPE_THETA = 10000.0
EPS = 1e-6
LOG2E = 1.4426950408889634

LANES = 128
BF16_SUBLANES = 16
COND_ROWS = 16
VMEM_LIMIT = 56 * 1024 * 1024

TM = 512
TM_OUT = 512
TQ = 128
KEY_CHUNK = 256
Q_PAR = 4
TR = 512
BM = 512
ADA_TN = 1024
FOURIER_TR = 512
SLAB = 512


def _cparams(n_axes, flags=None):
    return pltpu.CompilerParams(dimension_semantics=("arbitrary",) * n_axes,
                                vmem_limit_bytes=VMEM_LIMIT, flags=flags)


def _dot(a, b):
    return jnp.dot(a, b, preferred_element_type=F32)


def _dot_nt(a, b):
    return lax.dot_general(a, b, (((1,), (1,)), ((), ())), preferred_element_type=F32)


def _split_bf16(x):
    hi = x.astype(BF16)
    lo = (x - hi.astype(F32)).astype(BF16)
    return hi, lo


def _ada_kernel(c_ref, w_ref, b_ref, o_ref):
    c = c_ref[...]
    c = c / (1.0 + jnp.exp(-c))
    c_hi, c_lo = _split_bf16(c)
    w_hi, w_lo = _split_bf16(w_ref[0])
    acc = _dot(c_hi, w_hi) + _dot(c_lo, w_hi) + _dot(c_hi, w_lo)
    o_ref[0] = acc + b_ref[0]


def _ada_call(cond, w_ada, b_ada):
    depth, d, n = w_ada.shape
    return pl.pallas_call(
        _ada_kernel,
        grid=(depth, n // ADA_TN),
        in_specs=[pl.BlockSpec((COND_ROWS, d), lambda l, j: (0, 0)),
                  pl.BlockSpec((1, d, ADA_TN), lambda l, j: (l, 0, j)),
                  pl.BlockSpec((1, 1, ADA_TN), lambda l, j: (l, 0, j))],
        out_specs=pl.BlockSpec((1, COND_ROWS, ADA_TN), lambda l, j: (l, 0, j)),
        out_shape=jax.ShapeDtypeStruct((depth, COND_ROWS, n), F32),
        compiler_params=_cparams(2),
        name="ada",
    )(cond, w_ada, b_ada.reshape(depth, 1, n))


def _proj_kernel(x_ref, mod_ref, g_ref, w_ref, cs_ref, cos_ref, sin_ref,
                 q0_ref, q1_ref, kvb_ref, kvf_ref, u_ref, gv_ref, zcs_ref,
                 *, n_rope_tiles, q_scale):
    i = pl.program_id(0)
    d = D_MODEL
    x = x_ref[...]
    ms = jnp.mean(x * x, axis=-1, keepdims=True)
    m = mod_ref[0]
    sh = m[:, 0:d]
    sc = m[:, d:2 * d]
    h = (x * lax.rsqrt(ms + EPS) * g_ref[...]) * (1.0 + sc) + sh
    hb = h.astype(BF16)

    use_rope = i < n_rope_tiles
    cos = jnp.where(use_rope, cos_ref[...], 1.0)
    sin = jnp.where(use_rope, sin_ref[...], 0.0)
    lane = lax.broadcasted_iota(I32, cos.shape, 1)
    first_half = (lane % MAP_DIM) < ROPE_HALF
    lo_map = lane < MAP_DIM

    def rope(t):
        swapped = jnp.where(first_half,
                            pltpu.roll(t, HEAD_COLS - ROPE_HALF, 1),
                            pltpu.roll(t, ROPE_HALF, 1))
        return t * cos + swapped * sin

    for hh in range(A_HEADS):
        c0 = hh * HEAD_COLS
        c1 = c0 + HEAD_COLS
        qh = rope(_dot(hb, w_ref[:, c0:c1])) * q_scale
        q0_ref[:, c0:c1] = jnp.where(lo_map, qh, 0.0).astype(BF16)
        q1_ref[:, c0:c1] = jnp.where(lo_map, 0.0, qh).astype(BF16)
        kh = _dot(hb, w_ref[:, A_WIDTH + c0:A_WIDTH + c1])
        kvf_ref[:, c0:c1] = kh
        kvb_ref[:, c0:c1] = rope(kh).astype(BF16)
    v = _dot(hb, w_ref[:, 2 * A_WIDTH:3 * A_WIDTH])
    kvf_ref[:, A_WIDTH:] = v
    kvb_ref[:, A_WIDTH:] = v.astype(BF16)
    o = 3 * A_WIDTH
    u_ref[...] = _dot(hb, w_ref[:, o:o + B_WIDTH]).astype(BF16)
    o += B_WIDTH
    gv_ref[...] = _dot(hb, w_ref[:, o:o + B_WIDTH]).astype(BF16)
    o += B_WIDTH
    zb = _dot(hb, w_ref[:, o:o + C_WIDTH]).astype(BF16)
    for g in range(C_GROUPS):
        c0 = g * C_GROUP_DIM
        c1 = c0 + C_GROUP_DIM
        zz = _dot(zb[:, c0:c1], cs_ref[...])
        zcs_ref[:, c0:c1] = zz[:, :C_GROUP_DIM].astype(BF16)
        zcs_ref[:, C_WIDTH + c0:C_WIDTH + c1] = zz[:, C_GROUP_DIM:].astype(BF16)


def _proj_call(x, mod_l, norm_g, w_in_b, cs_chan, cos_t, sin_t, geo):
    t_tok, d = x.shape
    n_tiles = t_tok // TM
    ns_tiles = geo["t_sample"] // TM
    rope_blocks = geo["sample_seq"] // TM
    n_prompt = t_tok - geo["t_sample"]
    sample_seq = geo["sample_seq"]
    n_sb = geo["n_sample_b"]

    def mod_row(i):
        return (jnp.minimum((i * TM) // sample_seq, n_sb), 0, 0)

    kern = functools.partial(_proj_kernel, n_rope_tiles=ns_tiles,
                             q_scale=float(MAP_DIM ** -0.5 * LOG2E))
    row_blk = lambda i: (i, 0)
    const = lambda i: (0, 0)
    return pl.pallas_call(
        kern,
        grid=(n_tiles,),
        in_specs=[pl.BlockSpec((TM, d), row_blk),
                  pl.BlockSpec((1, 1, 6 * d), mod_row),
                  pl.BlockSpec((1, d), const),
                  pl.BlockSpec((d, IN_COLS), const, pipeline_mode=pl.Buffered(1)),
                  pl.BlockSpec((C_GROUP_DIM, 2 * C_GROUP_DIM), const),
                  pl.BlockSpec((TM, HEAD_COLS), lambda i: (i % rope_blocks, 0)),
                  pl.BlockSpec((TM, HEAD_COLS), lambda i: (i % rope_blocks, 0))],
        out_specs=[pl.BlockSpec((TM, A_WIDTH), row_blk),
                   pl.BlockSpec((TM, A_WIDTH), row_blk),
                   pl.BlockSpec((TM, 2 * A_WIDTH), row_blk),
                   pl.BlockSpec((TM, 2 * A_WIDTH), lambda i: (jnp.maximum(i - ns_tiles, 0), 0)),
                   pl.BlockSpec((TM, B_WIDTH), row_blk),
                   pl.BlockSpec((TM, B_WIDTH), row_blk),
                   pl.BlockSpec((TM, 2 * C_WIDTH), row_blk)],
        out_shape=[jax.ShapeDtypeStruct((t_tok, A_WIDTH), BF16),
                   jax.ShapeDtypeStruct((t_tok, A_WIDTH), BF16),
                   jax.ShapeDtypeStruct((t_tok, 2 * A_WIDTH), BF16),
                   jax.ShapeDtypeStruct((n_prompt, 2 * A_WIDTH), F32),
                   jax.ShapeDtypeStruct((t_tok, B_WIDTH), BF16),
                   jax.ShapeDtypeStruct((t_tok, B_WIDTH), BF16),
                   jax.ShapeDtypeStruct((t_tok, 2 * C_WIDTH), BF16)],
        compiler_params=_cparams(1),
        name="proj",
    )(x, mod_l, norm_g, w_in_b, cs_chan, cos_t, sin_t)


def _attn_kernel(*refs, has_cache, lam_init, n_seq, past, tq, heads):
    if has_cache:
        (q0_ref, q1_ref, k_ref, v_ref, ck_ref, cv_ref, lam_ref, sw_ref, o_ref,
         s_scr, kc_scr, vc_scr) = refs
        kc_scr[...] = ck_ref[0, 0].astype(BF16)
        vc_scr[...] = cv_ref[0, 0].astype(BF16)
        cache_chunks = [(kc_scr, vc_scr, c * KEY_CHUNK) for c in range(past // KEY_CHUNK)]
    else:
        q0_ref, q1_ref, k_ref, v_ref, lam_ref, sw_ref, o_ref, s_scr = refs
        cache_chunks = []
    new_chunks = [(k_ref, v_ref, c * KEY_CHUNK) for c in range(n_seq // KEY_CHUNK)]
    lam = lam_ref[...]
    lam_val = (jnp.exp(jnp.sum(lam[0:1] * lam[1:2], axis=-1, keepdims=True))
               - jnp.exp(jnp.sum(lam[2:3] * lam[3:4], axis=-1, keepdims=True)) + lam_init)
    sw = sw_ref[...] * (1.0 - lam_init)

    def block(r0, hh, s_buf):
        c0 = hh * HEAD_COLS
        c1 = c0 + HEAD_COLS
        chunks = ([(kr, vr, k0, 0, HEAD_COLS) for kr, vr, k0 in cache_chunks]
                  + [(kr, vr, k0, c0, c1) for kr, vr, k0 in new_chunks])
        qq = jnp.concatenate([q0_ref[pl.ds(r0, tq), c0:c1], q1_ref[pl.ds(r0, tq), c0:c1]], axis=0)
        mrun = None
        for ci, (kr, _, k0, a0, a1) in enumerate(chunks):
            s_c = _dot_nt(qq, kr[k0:k0 + KEY_CHUNK, a0:a1])
            s_buf[:, ci * KEY_CHUNK:(ci + 1) * KEY_CHUNK] = s_c
            for h0 in range(0, KEY_CHUNK, LANES):
                part = s_c[:, h0:h0 + LANES]
                mrun = part if mrun is None else jnp.maximum(mrun, part)
        m = jnp.max(mrun, axis=-1, keepdims=True)
        acc = jnp.zeros((2 * tq, HEAD_COLS), F32)
        lrun = jnp.zeros((2 * tq, LANES), F32)
        for ci, (_, vr, k0, a0, a1) in enumerate(chunks):
            p = jnp.exp2(s_buf[:, ci * KEY_CHUNK:(ci + 1) * KEY_CHUNK] - m)
            for h0 in range(0, KEY_CHUNK, LANES):
                lrun = lrun + p[:, h0:h0 + LANES]
            acc = acc + _dot(p.astype(BF16), vr[k0:k0 + KEY_CHUNK, a0:a1])
        r = 1.0 / jnp.sum(lrun, axis=-1, keepdims=True)
        o = acc[:tq] * r[:tq] - acc[tq:] * (r[tq:] * lam_val)
        o = o * lax.rsqrt(jnp.mean(o * o, axis=-1, keepdims=True) + EPS) * sw
        o_ref[pl.ds(r0, tq), c0:c1] = o.astype(BF16)

    n_par = s_scr.shape[0] // heads

    def body(j, carry):
        for hh in range(heads):
            for u in range(n_par):
                block(pl.multiple_of((j * n_par + u) * tq, tq), hh, s_scr.at[hh * n_par + u])
        return carry

    lax.fori_loop(0, n_seq // (tq * n_par), body, 0)


def _attn_call(q0, q1, kvb, lam_l, subln_l, lam_init, n_batch, n_seq, row_off, cache=None):
    blk_off = row_off // n_seq
    tq = min(TQ, n_seq)
    has_cache = cache is not None
    heads = 1 if has_cache else A_HEADS
    n_hsteps = A_HEADS // heads
    cols = heads * HEAD_COLS
    q_spec = pl.BlockSpec((n_seq, cols), lambda b, h: (blk_off + b, h))
    in_specs = [q_spec, q_spec,
                pl.BlockSpec((n_seq, cols), lambda b, h: (blk_off + b, h)),
                pl.BlockSpec((n_seq, cols), lambda b, h: (blk_off + b, n_hsteps + h))]
    args = [q0, q1, kvb, kvb]
    past = 0
    if has_cache:
        ck, cv, layer = cache
        past = ck.shape[2]
        c_spec = pl.BlockSpec((1, 1, past, HEAD_COLS), lambda b, h: (b, layer, 0, h))
        in_specs += [c_spec, c_spec]
        args += [ck, cv]
    n_par = min(Q_PAR, n_seq // tq)
    scratch = [pltpu.VMEM((heads * n_par, 2 * tq, past + n_seq), F32)]
    if has_cache:
        scratch += [pltpu.VMEM((past, HEAD_COLS), BF16), pltpu.VMEM((past, HEAD_COLS), BF16)]
    in_specs += [pl.BlockSpec((4, MAP_DIM), lambda b, h: (0, 0)),
                 pl.BlockSpec((1, HEAD_COLS), lambda b, h: (0, 0))]
    args += [lam_l, subln_l]
    kern = functools.partial(_attn_kernel, has_cache=has_cache, lam_init=float(lam_init),
                             n_seq=n_seq, past=past, tq=tq, heads=heads)
    return pl.pallas_call(
        kern,
        grid=(n_batch, n_hsteps),
        in_specs=in_specs,
        out_specs=pl.BlockSpec((n_seq, cols), lambda b, h: (b, h)),
        out_shape=jax.ShapeDtypeStruct((n_batch * n_seq, A_WIDTH), BF16),
        scratch_shapes=scratch,
        compiler_params=_cparams(2),
        name="attn_cache" if has_cache else "attn",
    )(*args)


def _fourier_kernel(c_ref, s_ref, z_ref, o_ref, *, scale):
    zc = z_ref[:, :C_WIDTH]
    zs = z_ref[:, C_WIDTH:]
    y = _dot(c_ref[...], zc) - _dot(s_ref[...], zs)
    o_ref[...] = (y * scale).astype(BF16)


def _fourier_call(zcs, cos_n, sin_n, n_batch, n_seq, row_off):
    tr = min(FOURIER_TR, n_seq)
    n_f = n_seq // tr
    blk_off = row_off // n_seq
    kern = functools.partial(_fourier_kernel, scale=float((n_seq * C_GROUP_DIM) ** -0.5))
    return pl.pallas_call(
        kern,
        grid=(n_batch, n_f),
        in_specs=[pl.BlockSpec((tr, n_seq), lambda b, f: (f, 0)),
                  pl.BlockSpec((tr, n_seq), lambda b, f: (f, 0)),
                  pl.BlockSpec((n_seq, 2 * C_WIDTH), lambda b, f: (blk_off + b, 0))],
        out_specs=pl.BlockSpec((tr, C_WIDTH), lambda b, f: (b * n_f + f, 0)),
        out_shape=jax.ShapeDtypeStruct((n_batch * n_seq, C_WIDTH), BF16),
        compiler_params=_cparams(2),
        name="fourier",
    )(cos_n, sin_n, zcs)


def _out_kernel(oa_s_ref, oa_p_ref, oc_s_ref, oc_p_ref, u_ref, gv_ref, x_ref, mod_ref,
                sgw_ref, sgb_ref, w_ref, g_ref, rw_hi_ref, rw_lo_ref, rb_ref,
                xo_ref, h2_ref, lg_ref, mix_scr, *, ns_tiles):
    i = pl.program_id(0)
    d = D_MODEL

    @pl.when(i < ns_tiles)
    def _():
        mix_scr[:, :A_WIDTH] = oa_s_ref[...]
        mix_scr[:, A_WIDTH + B_WIDTH:] = oc_s_ref[...]

    @pl.when(i >= ns_tiles)
    def _():
        mix_scr[:, :A_WIDTH] = oa_p_ref[...]
        mix_scr[:, A_WIDTH + B_WIDTH:] = oc_p_ref[...]

    for c in range(x_ref.shape[0] // B_CHUNK):
        r0 = c * B_CHUNK
        for g in range(B_GROUPS):
            c0 = g * (B_WIDTH // B_GROUPS)
            c1 = c0 + B_WIDTH // B_GROUPS
            gate = _dot(sgw_ref[g], gv_ref[r0:r0 + B_CHUNK, c0:c1]) + sgb_ref[:, g:g + 1]
            ob = u_ref[r0:r0 + B_CHUNK, c0:c1].astype(F32) * gate
            mix_scr[r0:r0 + B_CHUNK, A_WIDTH + c0:A_WIDTH + c1] = ob.astype(BF16)

    m = mod_ref[0]
    g1 = m[:, 2 * d:3 * d]
    sh2 = m[:, 3 * d:4 * d]
    sc2 = m[:, 4 * d:5 * d]
    y = _dot(mix_scr[...], w_ref[...])
    xn = x_ref[...] + g1 * y
    xo_ref[...] = xn
    ms = jnp.mean(xn * xn, axis=-1, keepdims=True)
    h2 = (xn * lax.rsqrt(ms + EPS) * g_ref[...]) * (1.0 + sc2) + sh2
    h_hi, h_lo = _split_bf16(h2)
    h2_ref[...] = h_hi
    logits = (_dot_nt(rw_hi_ref[...], h_hi) + _dot_nt(rw_hi_ref[...], h_lo)
              + _dot_nt(rw_lo_ref[...], h_hi))
    lg_ref[...] = logits + rb_ref[...]


def _out_call(oa_s, oa_p, oc_s, oc_p, u, gv, x, mod_l, sgw_b, sgb_t, w_out_b, norm_g,
              rw_hi, rw_lo, rb_col, geo):
    t_tok, d = x.shape
    tm = TM_OUT
    n_tiles = t_tok // tm
    ns_tiles = geo["t_sample"] // tm
    sample_seq = geo["sample_seq"]
    n_sb = geo["n_sample_b"]

    def mod_row(i):
        return (jnp.minimum((i * tm) // sample_seq, n_sb), 0, 0)

    row_blk = lambda i: (i, 0)
    s_blk = lambda i: (jnp.minimum(i, ns_tiles - 1), 0)
    p_blk = lambda i: (jnp.maximum(i - ns_tiles, 0), 0)
    const2 = lambda i: (0, 0)
    kern = functools.partial(_out_kernel, ns_tiles=ns_tiles)
    return pl.pallas_call(
        kern,
        grid=(n_tiles,),
        in_specs=[pl.BlockSpec((tm, A_WIDTH), s_blk),
                  pl.BlockSpec((tm, A_WIDTH), p_blk),
                  pl.BlockSpec((tm, C_WIDTH), s_blk),
                  pl.BlockSpec((tm, C_WIDTH), p_blk),
                  pl.BlockSpec((tm, B_WIDTH), row_blk),
                  pl.BlockSpec((tm, B_WIDTH), row_blk),
                  pl.BlockSpec((tm, d), row_blk),
                  pl.BlockSpec((1, 1, 6 * d), mod_row),
                  pl.BlockSpec((B_GROUPS, B_CHUNK, B_CHUNK), lambda i: (0, 0, 0)),
                  pl.BlockSpec((B_CHUNK, B_GROUPS), const2),
                  pl.BlockSpec((d, d), const2, pipeline_mode=pl.Buffered(1)),
                  pl.BlockSpec((1, d), const2),
                  pl.BlockSpec((N_EXPERTS, d), const2),
                  pl.BlockSpec((N_EXPERTS, d), const2),
                  pl.BlockSpec((N_EXPERTS, 1), const2)],
        out_specs=[pl.BlockSpec((tm, d), row_blk),
                   pl.BlockSpec((tm, d), row_blk),
                   pl.BlockSpec((N_EXPERTS, tm), lambda i: (0, i))],
        out_shape=[jax.ShapeDtypeStruct((t_tok, d), F32),
                   jax.ShapeDtypeStruct((t_tok, d), BF16),
                   jax.ShapeDtypeStruct((N_EXPERTS, t_tok), F32)],
        scratch_shapes=[pltpu.VMEM((tm, d), BF16)],
        compiler_params=_cparams(1),
        name="out",
    )(oa_s, oa_p, oc_s, oc_p, u, gv, x, mod_l, sgw_b, sgb_t, w_out_b, norm_g,
      rw_hi, rw_lo, rb_col)


def _route_kernel(lg_ref, idx_ref, gate_ref, rank_ref, cnt_ref):
    lg = lg_ref[...]
    n_e, tr = lg.shape
    eidx = lax.broadcasted_iota(I32, lg.shape, 0).astype(F32)
    work = lg
    vals = []
    sels = []
    for _ in range(TOP_K):
        mk = jnp.max(work, axis=0, keepdims=True)
        ik = jnp.min(jnp.where(work == mk, eidx, float(n_e)), axis=0, keepdims=True)
        hit = eidx == ik
        work = jnp.where(hit, -jnp.inf, work)
        vals.append(mk)
        sels.append(hit)
    es = [jnp.exp(v - vals[0]) for v in vals]
    inv = 1.0 / (es[0] + es[1] + es[2] + es[3])
    onehot = jnp.where(sels[0] | sels[1] | sels[2] | sels[3], 1.0, 0.0)
    before = (lax.broadcasted_iota(I32, (tr, tr), 0) < lax.broadcasted_iota(I32, (tr, tr), 1))
    upper = jnp.where(before, 1.0, 0.0).astype(BF16)
    prefix = _dot(onehot.astype(BF16), upper)
    for k in range(TOP_K):
        idx_ref[k:k + 1, :] = jnp.sum(jnp.where(sels[k], eidx, 0.0), axis=0, keepdims=True).astype(I32)
        rank_ref[k:k + 1, :] = jnp.sum(jnp.where(sels[k], prefix, 0.0), axis=0, keepdims=True).astype(I32)
        gate_ref[k:k + 1, :] = es[k] * inv
    cnt_ref[0] = jnp.sum(onehot, axis=1, keepdims=True).astype(I32)


def _route_call(logits_t):
    n_e, t_tok = logits_t.shape
    n_tiles = t_tok // TR
    tok_blk = lambda i: (0, i)
    return pl.pallas_call(
        _route_kernel,
        grid=(n_tiles,),
        in_specs=[pl.BlockSpec((n_e, TR), tok_blk)],
        out_specs=[pl.BlockSpec((TOP_K, TR), tok_blk),
                   pl.BlockSpec((TOP_K, TR), tok_blk),
                   pl.BlockSpec((TOP_K, TR), tok_blk),
                   pl.BlockSpec((1, n_e, 1), lambda i: (i, 0, 0))],
        out_shape=[jax.ShapeDtypeStruct((TOP_K, t_tok), I32),
                   jax.ShapeDtypeStruct((TOP_K, t_tok), F32),
                   jax.ShapeDtypeStruct((TOP_K, t_tok), I32),
                   jax.ShapeDtypeStruct((n_tiles, n_e, 1), I32)],
        compiler_params=_cparams(1),
        name="route",
    )(logits_t)


def _rows_local():
    return TOP_K * TR + N_EXPERTS * BF16_SUBLANES


def _plan(counts, t_tok):
    n_tiles = t_tok // TR
    g = BF16_SUBLANES
    rows_loc = _rows_local()
    n_chunk_max = rows_loc // g
    counts = counts.reshape(n_tiles, N_EXPERTS)
    padded = (counts + g - 1) // g * g
    loc_end = jnp.cumsum(padded, axis=1)
    loc_start = loc_end - padded
    tot = jnp.sum(padded, axis=0)
    region = (tot + BM - 1) // BM * BM
    reg_end = jnp.cumsum(region)
    reg_start = reg_end - region
    seg_off = reg_start[None, :] + jnp.cumsum(padded, axis=0) - padded
    n_chunks = loc_end[:, -1] // g
    chunk_row = jnp.arange(n_chunk_max, dtype=I32) * g
    chunk_e = jnp.sum((loc_end[:, None, :] <= chunk_row[None, :, None]).astype(I32), axis=-1)
    chunk_e = jnp.minimum(chunk_e, N_EXPERTS - 1)
    owner = chunk_e[:, :, None] == jnp.arange(N_EXPERTS, dtype=I32)[None, None, :]
    seg_shift = jnp.sum(jnp.where(owner, (seg_off - loc_start)[:, None, :], 0), axis=-1)
    chunk_dst = seg_shift + chunk_row[None, :]
    n_rows_max = (TOP_K * t_tok + n_tiles * N_EXPERTS * (g - 1) + N_EXPERTS * (BM - g))
    n_blocks_max = -(-n_rows_max // BM)
    n_used = reg_end[-1] // BM
    blk_row = jnp.arange(n_blocks_max, dtype=I32) * BM
    blk_e = jnp.sum((reg_end[None, :] <= blk_row[:, None]).astype(I32), axis=-1)
    blk_e = jnp.minimum(blk_e, N_EXPERTS - 1)
    pad_start = reg_start + tot
    pad_chunks = (region - tot) // g
    return dict(loc_start=loc_start.reshape(-1).astype(I32),
                n_chunks=n_chunks.astype(I32),
                chunk_dst=chunk_dst.reshape(-1).astype(I32),
                blk_e=blk_e.astype(I32),
                n_used=n_used.reshape(1).astype(I32),
                pad_start=pad_start.astype(I32),
                pad_chunks=pad_chunks.astype(I32),
                n_blocks_max=n_blocks_max)


def _dispatch_kernel(ls_ref, nch_ref, cdst_ref, pst_ref, pch_ref, nused_ref,
                     h_ref, idx_ref, rank_ref, xs_hbm, loc_scr, zero_scr, sem, zsem,
                     *, n_blocks_max):
    i = pl.program_id(0)
    n_tiles = pl.num_programs(0)
    g = BF16_SUBLANES
    rows_loc = loc_scr.shape[1]
    n_chunk_max = rows_loc // g
    slot = i % 2

    def seg_copy(tile, c, sl):
        src = pl.multiple_of(c * g, g)
        dst = pl.multiple_of(cdst_ref[tile * n_chunk_max + c], g)
        return pltpu.make_async_copy(loc_scr.at[sl, pl.ds(src, g), :],
                                     xs_hbm.at[pl.ds(dst, g), :], sem.at[sl])

    def wait_tile(tile, sl):
        def wait(c, carry):
            seg_copy(tile, 0, sl).wait()
            return carry
        lax.fori_loop(0, nch_ref[tile], wait, 0)

    @pl.when(i >= 2)
    def _():
        wait_tile(i - 2, slot)

    idx = idx_ref[...]
    lrow = rank_ref[...]
    for e in range(N_EXPERTS):
        lrow = lrow + jnp.where(idx == e, ls_ref[i * N_EXPERTS + e], 0)

    def sort_slab(s, carry):
        r0 = pl.multiple_of(s * SLAB, SLAB)
        riota = lax.broadcasted_iota(I32, (SLAB, TR), 0) + r0
        sel = jnp.where(riota == lrow[0:1, :], 1.0, 0.0)
        for k in range(1, TOP_K):
            sel = jnp.where(riota == lrow[k:k + 1, :], 1.0, sel)
        loc_scr[slot, pl.ds(r0, SLAB), :] = _dot(sel.astype(BF16), h_ref[...]).astype(BF16)
        return carry

    lax.fori_loop(0, rows_loc // SLAB, sort_slab, 0)

    def start(c, carry):
        seg_copy(i, c, slot).start()
        return carry

    lax.fori_loop(0, nch_ref[i], start, 0)

    @pl.when(i == n_tiles - 1)
    def _():
        @pl.when(i >= 1)
        def _():
            wait_tile(i - 1, 1 - slot)

        wait_tile(i, slot)
        zero_scr[...] = jnp.zeros_like(zero_scr)

        def pad_copy(row):
            row = pl.multiple_of(row, g)
            return pltpu.make_async_copy(zero_scr.at[pl.ds(0, g), :], xs_hbm.at[pl.ds(row, g), :], zsem)

        def blk_copy(b):
            row = pl.multiple_of(b * BM, BM)
            return pltpu.make_async_copy(zero_scr, xs_hbm.at[pl.ds(row, BM), :], zsem)

        for e in range(N_EXPERTS):
            def pstart(j, carry, e=e):
                pad_copy(pst_ref[e] + j * g).start()
                return carry

            def pwait(j, carry, e=e):
                pad_copy(pst_ref[e] + j * g).wait()
                return carry

            lax.fori_loop(0, pch_ref[e], pstart, 0)
            lax.fori_loop(0, pch_ref[e], pwait, 0)

        def bstart(b, carry):
            blk_copy(b).start()
            return carry

        def bwait(b, carry):
            blk_copy(b).wait()
            return carry

        lax.fori_loop(nused_ref[0], n_blocks_max, bstart, 0)
        lax.fori_loop(nused_ref[0], n_blocks_max, bwait, 0)


def _dispatch_call(plan, h2, idx_t, rank_t):
    t_tok, d = h2.shape
    n_tiles = t_tok // TR
    rows_loc = _rows_local()
    n_blocks_max = plan["n_blocks_max"]
    kern = functools.partial(_dispatch_kernel, n_blocks_max=n_blocks_max)
    grid_spec = pltpu.PrefetchScalarGridSpec(
        num_scalar_prefetch=6,
        grid=(n_tiles,),
        in_specs=[pl.BlockSpec((TR, d), lambda i, *_: (i, 0)),
                  pl.BlockSpec((TOP_K, TR), lambda i, *_: (0, i)),
                  pl.BlockSpec((TOP_K, TR), lambda i, *_: (0, i))],
        out_specs=pl.BlockSpec(memory_space=pl.ANY),
        scratch_shapes=[pltpu.VMEM((2, rows_loc, d), BF16),
                        pltpu.VMEM((BM, d), BF16),
                        pltpu.SemaphoreType.DMA((2,)),
                        pltpu.SemaphoreType.DMA(())])
    return pl.pallas_call(
        kern,
        grid_spec=grid_spec,
        out_shape=jax.ShapeDtypeStruct((n_blocks_max * BM, d), BF16),
        compiler_params=_cparams(1),
        name="dispatch",
    )(plan["loc_start"], plan["n_chunks"], plan["chunk_dst"], plan["pad_start"],
      plan["pad_chunks"], plan["n_used"], h2, idx_t, rank_t)


def _expert_kernel(be_ref, nused_ref, x_ref, w1_ref, b1_ref, w2_ref, b2_ref, y_ref):
    b = pl.program_id(0)

    @pl.when(b < nused_ref[0])
    def _():
        gu = _dot(x_ref[...], w1_ref[0]) + b1_ref[0]
        gate = jnp.minimum(gu[:, :D_FF], SWIGLU_LIMIT)
        up = jnp.clip(gu[:, D_FF:], -SWIGLU_LIMIT, SWIGLU_LIMIT)
        act = (up + 1.0) * gate * (1.0 / (1.0 + jnp.exp(-SWIGLU_ALPHA * gate)))
        y = _dot(act.astype(BF16), w2_ref[0]) + b2_ref[0]
        y_ref[...] = y.astype(BF16)

    @pl.when(b >= nused_ref[0])
    def _():
        y_ref[...] = jnp.zeros_like(y_ref)


def _expert_call(plan, xs, w1_b, b1, w2_b, b2):
    n_rows, d = xs.shape
    n_blocks = n_rows // BM
    n_e = w1_b.shape[0]
    grid_spec = pltpu.PrefetchScalarGridSpec(
        num_scalar_prefetch=2,
        grid=(n_blocks,),
        in_specs=[pl.BlockSpec((BM, d), lambda b, be, nu: (jnp.minimum(b, nu[0] - 1), 0)),
                  pl.BlockSpec((1, d, 2 * D_FF), lambda b, be, nu: (be[b], 0, 0)),
                  pl.BlockSpec((1, 1, 2 * D_FF), lambda b, be, nu: (be[b], 0, 0)),
                  pl.BlockSpec((1, D_FF, d), lambda b, be, nu: (be[b], 0, 0)),
                  pl.BlockSpec((1, 1, d), lambda b, be, nu: (be[b], 0, 0))],
        out_specs=pl.BlockSpec((BM, d), lambda b, be, nu: (b, 0)))
    return pl.pallas_call(
        _expert_kernel,
        grid_spec=grid_spec,
        out_shape=jax.ShapeDtypeStruct((n_rows, d), BF16),
        compiler_params=_cparams(1),
        name="experts",
    )(plan["blk_e"], plan["n_used"], xs, w1_b, b1.reshape(n_e, 1, 2 * D_FF), w2_b,
      b2.reshape(n_e, 1, d))


def _combine_kernel(ls_ref, nch_ref, cdst_ref, x_ref, mod_ref, idx_ref, rank_ref, gate_ref,
                    fin_ref, ys_hbm, *rest, final, ns_tiles):
    if final:
        os_ref, op_ref, loc_scr, w_scr, sem = rest
    else:
        o_ref, loc_scr, w_scr, sem = rest
    i = pl.program_id(0)
    n_tiles = pl.num_programs(0)
    g = BF16_SUBLANES
    d = D_MODEL
    rows_loc = loc_scr.shape[1]
    n_chunk_max = rows_loc // g
    slot = i % 2

    def seg_copy(tile, c, sl):
        dst = pl.multiple_of(c * g, g)
        src = pl.multiple_of(cdst_ref[tile * n_chunk_max + c], g)
        return pltpu.make_async_copy(ys_hbm.at[pl.ds(src, g), :],
                                     loc_scr.at[sl, pl.ds(dst, g), :], sem.at[sl])

    def fetch_tile(tile, sl):
        def start(c, carry):
            seg_copy(tile, c, sl).start()
            return carry
        lax.fori_loop(0, nch_ref[tile], start, 0)

    @pl.when(i == 0)
    def _():
        loc_scr[...] = jnp.zeros_like(loc_scr)
        fetch_tile(0, 0)

    @pl.when(i + 1 < n_tiles)
    def _():
        fetch_tile(i + 1, 1 - slot)

    idx = idx_ref[...]
    lrow = rank_ref[...]
    for e in range(N_EXPERTS):
        lrow = lrow + jnp.where(idx == e, ls_ref[i * N_EXPERTS + e], 0)
    gates = gate_ref[...]
    for s in range(rows_loc // SLAB):
        ciota = lax.broadcasted_iota(I32, (TR, SLAB), 1) + s * SLAB
        wsel = jnp.where(ciota == lrow[:, 0:1], gates[:, 0:1], 0.0)
        for k in range(1, TOP_K):
            wsel = jnp.where(ciota == lrow[:, k:k + 1], gates[:, k:k + 1], wsel)
        w_scr[:, s * SLAB:(s + 1) * SLAB] = wsel.astype(BF16)

    def wait(c, carry):
        seg_copy(i, 0, slot).wait()
        return carry

    lax.fori_loop(0, nch_ref[i], wait, 0)

    y = _dot(w_scr[...], loc_scr[slot])
    g2 = mod_ref[0][:, 5 * d:6 * d]
    xn = x_ref[...] + g2 * y
    if final:
        ms = jnp.mean(xn * xn, axis=-1, keepdims=True)
        xn = xn * lax.rsqrt(ms + EPS) * fin_ref[...]

        @pl.when(i < ns_tiles)
        def _():
            os_ref[...] = xn

        @pl.when(i >= ns_tiles)
        def _():
            op_ref[...] = xn
    else:
        o_ref[...] = xn


def _combine_call(plan, x, mod_l, idx_c, rank_c, gate_c, final_g, ys, geo, final):
    t_tok, d = x.shape
    n_tiles = t_tok // TR
    rows_loc = _rows_local()
    sample_seq = geo["sample_seq"]
    n_sb = geo["n_sample_b"]

    def mod_row(i, *_):
        return (jnp.minimum((i * TR) // sample_seq, n_sb), 0, 0)

    row_blk = lambda i, *_: (i, 0)
    ns_tiles = geo["t_sample"] // TR
    if final:
        out_specs = [pl.BlockSpec((TR, d), lambda i, *_: (jnp.minimum(i, ns_tiles - 1), 0)),
                     pl.BlockSpec((TR, d), lambda i, *_: (jnp.maximum(i - ns_tiles, 0), 0))]
        out_shape = [jax.ShapeDtypeStruct((geo["t_sample"], d), F32),
                     jax.ShapeDtypeStruct((t_tok - geo["t_sample"], d), F32)]
    else:
        out_specs = pl.BlockSpec((TR, d), row_blk)
        out_shape = jax.ShapeDtypeStruct((t_tok, d), F32)
    kern = functools.partial(_combine_kernel, final=final, ns_tiles=ns_tiles)
    grid_spec = pltpu.PrefetchScalarGridSpec(
        num_scalar_prefetch=3,
        grid=(n_tiles,),
        in_specs=[pl.BlockSpec((TR, d), row_blk),
                  pl.BlockSpec((1, 1, 6 * d), mod_row),
                  pl.BlockSpec((TR, TOP_K), row_blk),
                  pl.BlockSpec((TR, TOP_K), row_blk),
                  pl.BlockSpec((TR, TOP_K), row_blk),
                  pl.BlockSpec((1, d), lambda i, *_: (0, 0)),
                  pl.BlockSpec(memory_space=pl.ANY)],
        out_specs=out_specs,
        scratch_shapes=[pltpu.VMEM((2, rows_loc, d), BF16),
                        pltpu.VMEM((TR, rows_loc), BF16),
                        pltpu.SemaphoreType.DMA((2,))])
    return pl.pallas_call(
        kern,
        grid_spec=grid_spec,
        out_shape=out_shape,
        compiler_params=_cparams(1),
        name="combine_final" if final else "combine",
    )(plan["loc_start"], plan["n_chunks"], plan["chunk_dst"], x, mod_l, idx_c, rank_c, gate_c,
      final_g, ys)


def _rope_tables(n_tok):
    rows = n_tok // GRID_W
    row = jnp.repeat(jnp.arange(rows, dtype=F32), GRID_W)
    col = jnp.tile(jnp.arange(GRID_W, dtype=F32), rows)
    pairs = ROPE_HALF // 2
    freqs = ROPE_THETA ** (-jnp.arange(pairs, dtype=F32) / pairs)
    ang = jnp.concatenate([row[:, None] * freqs, col[:, None] * freqs], axis=-1)
    cos, sin = jnp.cos(ang), jnp.sin(ang)
    cos_t = jnp.concatenate([cos, cos, cos, cos], axis=-1)
    sin_t = jnp.concatenate([-sin, sin, -sin, sin], axis=-1)
    return cos_t, sin_t


def _dft_tables(n):
    k = jnp.arange(n, dtype=I32)
    ang = ((k[:, None] * k[None, :]) % n).astype(F32) * (2.0 * math.pi / n)
    return jnp.cos(ang), jnp.sin(ang)


def _forward(x_prompt, x_sample, cache_k, cache_v, c, c_ctx, w_ada, b_ada, norm_mix, norm_ffn,
             w_in, lam, subln_w, sgu_w, sgu_b, w_out, router_w, router_b, moe_w1, moe_b1,
             moe_w2, moe_b2, final_norm):
    n_pb, p_seq, d = x_prompt.shape
    n_sb, s_seq, _ = x_sample.shape
    depth = w_ada.shape[0]
    past = cache_k.shape[2]
    t_sample = n_sb * s_seq
    t_prompt = n_pb * p_seq
    t_tok = t_sample + t_prompt
    geo = dict(t_sample=t_sample, sample_seq=s_seq, n_sample_b=n_sb)

    x = jnp.concatenate([x_sample.reshape(t_sample, d), x_prompt.reshape(t_prompt, d)], axis=0)
    cond = jnp.zeros((COND_ROWS, d), F32).at[:n_sb].set(c).at[n_sb].set(c_ctx)
    mod = _ada_call(cond, w_ada, b_ada)

    cos_t, sin_t = _rope_tables(s_seq)
    cc, sc = _dft_tables(C_GROUP_DIM)
    cs_chan = jnp.concatenate([cc, sc], axis=1).astype(BF16)
    cos_s, sin_s = [a.astype(BF16) for a in _dft_tables(s_seq)]
    cos_p, sin_p = [a.astype(BF16) for a in _dft_tables(p_seq)]
    ck = cache_k.reshape(n_sb, depth, past, A_WIDTH)
    cv = cache_v.reshape(n_sb, depth, past, A_WIDTH)

    new_k, new_v = [], []
    for l in range(depth):
        lam_init = 0.8 - 0.6 * math.exp(-0.3 * l)
        mod_l = mod[l].reshape(COND_ROWS, 1, 6 * d)
        q0, q1, kvb, kvf, u, gv, zcs = _proj_call(
            x, mod_l, norm_mix[l].reshape(1, d), w_in[l].astype(BF16), cs_chan, cos_t, sin_t, geo)
        new_k.append(kvf[:, :A_WIDTH].reshape(n_pb, p_seq, A_HEADS, HEAD_COLS))
        new_v.append(kvf[:, A_WIDTH:].reshape(n_pb, p_seq, A_HEADS, HEAD_COLS))
        subln_l = subln_w[l].reshape(1, HEAD_COLS)
        oa_s = _attn_call(q0, q1, kvb, lam[l], subln_l, lam_init, n_sb, s_seq, 0, cache=(ck, cv, l))
        oa_p = _attn_call(q0, q1, kvb, lam[l], subln_l, lam_init, n_pb, p_seq, t_sample)
        oc_s = _fourier_call(zcs, cos_s, sin_s, n_sb, s_seq, 0)
        oc_p = _fourier_call(zcs, cos_p, sin_p, n_pb, p_seq, t_sample)
        rw_hi, rw_lo = _split_bf16(router_w[l].T)
        x, h2, logits_t = _out_call(
            oa_s, oa_p, oc_s, oc_p, u, gv, x, mod_l, sgu_w[l].astype(BF16), sgu_b[l].T,
            w_out[l].astype(BF16), norm_ffn[l].reshape(1, d), rw_hi, rw_lo,
            router_b[l].reshape(N_EXPERTS, 1), geo)
        idx_t, gate_t, rank_t, counts = _route_call(logits_t)
        plan = _plan(counts, t_tok)
        xs = _dispatch_call(plan, h2, idx_t, rank_t)
        ys = _expert_call(plan, xs, moe_w1[l].astype(BF16), moe_b1[l], moe_w2[l].astype(BF16),
                          moe_b2[l])
        x = _combine_call(plan, x, mod_l, idx_t.T, rank_t.T, gate_t.T, final_norm.reshape(1, d),
                          ys, geo, final=(l == depth - 1))

    y_sample = x[0].reshape(n_sb, s_seq, d)
    y_prompt = x[1].reshape(n_pb, p_seq, d)
    return y_prompt, y_sample, jnp.stack(new_k, axis=1), jnp.stack(new_v, axis=1)


def kernel(x_prompt, x_sample, cache_k, cache_v, c, c_ctx, w_ada, b_ada, norm_mix, norm_ffn, w_in,
           lam, subln_w, sgu_w, sgu_b, w_out, router_w, router_b, moe_w1, moe_b1, moe_w2, moe_b2,
           final_norm):
    return _forward(x_prompt, x_sample, cache_k, cache_v, c, c_ctx, w_ada, b_ada, norm_mix,
                    norm_ffn, w_in, lam, subln_w, sgu_w, sgu_b, w_out, router_w, router_b,
                    moe_w1, moe_b1, moe_w2, moe_b2, final_norm)
```

```python
import functools
import math

import jax
import jax.numpy as jnp
from jax import lax
from jax.experimental import pallas as pl
from jax.experimental.pallas import tpu as pltpu

F32 = jnp.float32
BF16 = jnp.bfloat16
I32 = jnp.int32

D_MODEL = 2048
A_WIDTH = 1024
HEAD_COLS = 128
A_HEADS = 8
MAP_DIM = 64
ROPE_HALF = 32
B_WIDTH = 512
B_CHUNK = 128
B_GROUPS = 4
C_WIDTH = 512
C_GROUPS = 4
C_GROUP_DIM = 128
IN_COLS = 3 * A_WIDTH + 2 * B_WIDTH + C_WIDTH
N_EXPERTS = 32
TOP_K = 4
D_FF = 1024
SWIGLU_LIMIT = 7.0
SWIGLU_ALPHA = 1.702
GRID_W = 64
ROPE_THETA = 10000.0
EPS = 1e-6
LOG2E = 1.4426950408889634

LANES = 128
BF16_SUBLANES = 16
COND_ROWS = 16
VMEM_LIMIT = 56 * 1024 * 1024

TM = 512
TM_OUT = 512
TQ = 128
KEY_CHUNK = 256
Q_PAR = 4
TR = 512
BM = 512
ADA_TN = 1024
FOURIER_TR = 512
SLAB = 512


def _cparams(n_axes, flags=None):
    return pltpu.CompilerParams(dimension_semantics=("arbitrary",) * n_axes,
                                vmem_limit_bytes=VMEM_LIMIT, flags=flags)


def _dot(a, b):
    return jnp.dot(a, b, preferred_element_type=F32)


def _dot_nt(a, b):
    return lax.dot_general(a, b, (((1,), (1,)), ((), ())), preferred_element_type=F32)


def _split_bf16(x):
    hi = x.astype(BF16)
    lo = (x - hi.astype(F32)).astype(BF16)
    return hi, lo


def _ada_kernel(c_ref, w_ref, b_ref, o_ref):
    c = c_ref[...]
    c = c / (1.0 + jnp.exp(-c))
    c_hi, c_lo = _split_bf16(c)
    w_hi, w_lo = _split_bf16(w_ref[0])
    acc = _dot(c_hi, w_hi) + _dot(c_lo, w_hi) + _dot(c_hi, w_lo)
    o_ref[0] = acc + b_ref[0]


def _ada_call(cond, w_ada, b_ada):
    depth, d, n = w_ada.shape
    return pl.pallas_call(
        _ada_kernel,
        grid=(depth, n // ADA_TN),
        in_specs=[pl.BlockSpec((COND_ROWS, d), lambda l, j: (0, 0)),
                  pl.BlockSpec((1, d, ADA_TN), lambda l, j: (l, 0, j)),
                  pl.BlockSpec((1, 1, ADA_TN), lambda l, j: (l, 0, j))],
        out_specs=pl.BlockSpec((1, COND_ROWS, ADA_TN), lambda l, j: (l, 0, j)),
        out_shape=jax.ShapeDtypeStruct((depth, COND_ROWS, n), F32),
        compiler_params=_cparams(2),
        name="ada",
    )(cond, w_ada, b_ada.reshape(depth, 1, n))


def _proj_kernel(x_ref, mod_ref, g_ref, w_ref, cs_ref, cos_ref, sin_ref,
                 q0_ref, q1_ref, kvb_ref, kvf_ref, u_ref, gv_ref, zcs_ref,
                 *, n_rope_tiles, q_scale):
    i = pl.program_id(0)
    d = D_MODEL
    x = x_ref[...]
    ms = jnp.mean(x * x, axis=-1, keepdims=True)
    m = mod_ref[0]
    sh = m[:, 0:d]
    sc = m[:, d:2 * d]
    h = (x * lax.rsqrt(ms + EPS) * g_ref[...]) * (1.0 + sc) + sh
    hb = h.astype(BF16)

    use_rope = i < n_rope_tiles
    cos = jnp.where(use_rope, cos_ref[...], 1.0)
    sin = jnp.where(use_rope, sin_ref[...], 0.0)
    lane = lax.broadcasted_iota(I32, cos.shape, 1)
    first_half = (lane % MAP_DIM) < ROPE_HALF
    lo_map = lane < MAP_DIM

    def rope(t):
        swapped = jnp.where(first_half,
                            pltpu.roll(t, HEAD_COLS - ROPE_HALF, 1),
                            pltpu.roll(t, ROPE_HALF, 1))
        return t * cos + swapped * sin

    for hp in range(0, A_HEADS, 2):
        p0 = hp * HEAD_COLS
        p1 = p0 + 2 * HEAD_COLS
        q_pair = _dot(hb, w_ref[:, p0:p1])
        k_pair = _dot(hb, w_ref[:, A_WIDTH + p0:A_WIDTH + p1])
        kvf_ref[:, p0:p1] = k_pair
        for hh in range(2):
            c0 = p0 + hh * HEAD_COLS
            c1 = c0 + HEAD_COLS
            qh = rope(q_pair[:, hh * HEAD_COLS:(hh + 1) * HEAD_COLS]) * q_scale
            q0_ref[:, c0:c1] = jnp.where(lo_map, qh, 0.0).astype(BF16)
            q1_ref[:, c0:c1] = jnp.where(lo_map, 0.0, qh).astype(BF16)
            kvb_ref[:, c0:c1] = rope(k_pair[:, hh * HEAD_COLS:(hh + 1) * HEAD_COLS]).astype(BF16)
    v = _dot(hb, w_ref[:, 2 * A_WIDTH:3 * A_WIDTH])
    kvf_ref[:, A_WIDTH:] = v
    kvb_ref[:, A_WIDTH:] = v.astype(BF16)
    o = 3 * A_WIDTH
    u_ref[...] = _dot(hb, w_ref[:, o:o + B_WIDTH]).astype(BF16)
    o += B_WIDTH
    gv_ref[...] = _dot(hb, w_ref[:, o:o + B_WIDTH]).astype(BF16)
    o += B_WIDTH
    zb = _dot(hb, w_ref[:, o:o + C_WIDTH]).astype(BF16)
    for g in range(C_GROUPS):
        c0 = g * C_GROUP_DIM
        c1 = c0 + C_GROUP_DIM
        zz = _dot(zb[:, c0:c1], cs_ref[...])
        zcs_ref[:, c0:c1] = zz[:, :C_GROUP_DIM].astype(BF16)
        zcs_ref[:, C_WIDTH + c0:C_WIDTH + c1] = zz[:, C_GROUP_DIM:].astype(BF16)


def _proj_call(x, mod_l, norm_g, w_in_b, cs_chan, cos_t, sin_t, geo):
    t_tok, d = x.shape
    n_tiles = t_tok // TM
    ns_tiles = geo["t_sample"] // TM
    rope_blocks = geo["sample_seq"] // TM
    n_prompt = t_tok - geo["t_sample"]
    sample_seq = geo["sample_seq"]
    n_sb = geo["n_sample_b"]

    def mod_row(i):
        return (jnp.minimum((i * TM) // sample_seq, n_sb), 0, 0)

    kern = functools.partial(_proj_kernel, n_rope_tiles=ns_tiles,
                             q_scale=float(MAP_DIM ** -0.5 * LOG2E))
    row_blk = lambda i: (i, 0)
    const = lambda i: (0, 0)
    return pl.pallas_call(
        kern,
        grid=(n_tiles,),
        in_specs=[pl.BlockSpec((TM, d), row_blk),
                  pl.BlockSpec((1, 1, 6 * d), mod_row),
                  pl.BlockSpec((1, d), const),
                  pl.BlockSpec((d, IN_COLS), const, pipeline_mode=pl.Buffered(1)),
                  pl.BlockSpec((C_GROUP_DIM, 2 * C_GROUP_DIM), const),
                  pl.BlockSpec((TM, HEAD_COLS), lambda i: (i % rope_blocks, 0)),
                  pl.BlockSpec((TM, HEAD_COLS), lambda i: (i % rope_blocks, 0))],
        out_specs=[pl.BlockSpec((TM, A_WIDTH), row_blk),
                   pl.BlockSpec((TM, A_WIDTH), row_blk),
                   pl.BlockSpec((TM, 2 * A_WIDTH), row_blk),
                   pl.BlockSpec((TM, 2 * A_WIDTH), lambda i: (jnp.maximum(i - ns_tiles, 0), 0)),
                   pl.BlockSpec((TM, B_WIDTH), row_blk),
                   pl.BlockSpec((TM, B_WIDTH), row_blk),
                   pl.BlockSpec((TM, 2 * C_WIDTH), row_blk)],
        out_shape=[jax.ShapeDtypeStruct((t_tok, A_WIDTH), BF16),
                   jax.ShapeDtypeStruct((t_tok, A_WIDTH), BF16),
                   jax.ShapeDtypeStruct((t_tok, 2 * A_WIDTH), BF16),
                   jax.ShapeDtypeStruct((n_prompt, 2 * A_WIDTH), F32),
                   jax.ShapeDtypeStruct((t_tok, B_WIDTH), BF16),
                   jax.ShapeDtypeStruct((t_tok, B_WIDTH), BF16),
                   jax.ShapeDtypeStruct((t_tok, 2 * C_WIDTH), BF16)],
        compiler_params=_cparams(1),
        name="proj",
    )(x, mod_l, norm_g, w_in_b, cs_chan, cos_t, sin_t)


def _attn_kernel(*refs, has_cache, lam_init, n_seq, past, tq, heads):
    if has_cache:
        (q0_ref, q1_ref, k_ref, v_ref, ck_ref, cv_ref, lam_ref, sw_ref, o_ref,
         s_scr, kc_scr, vc_scr) = refs
        kc_scr[...] = ck_ref[0, 0].astype(BF16)
        vc_scr[...] = cv_ref[0, 0].astype(BF16)
        cache_chunks = [(kc_scr, vc_scr, c * KEY_CHUNK) for c in range(past // KEY_CHUNK)]
    else:
        q0_ref, q1_ref, k_ref, v_ref, lam_ref, sw_ref, o_ref, s_scr = refs
        cache_chunks = []
    new_chunks = [(k_ref, v_ref, c * KEY_CHUNK) for c in range(n_seq // KEY_CHUNK)]
    lam = lam_ref[...]
    lam_val = (jnp.exp(jnp.sum(lam[0:1] * lam[1:2], axis=-1, keepdims=True))
               - jnp.exp(jnp.sum(lam[2:3] * lam[3:4], axis=-1, keepdims=True)) + lam_init)
    sw = sw_ref[...] * (1.0 - lam_init)

    def block(r0, hh, s_buf):
        c0 = hh * HEAD_COLS
        c1 = c0 + HEAD_COLS
        chunks = ([(kr, vr, k0, 0, HEAD_COLS) for kr, vr, k0 in cache_chunks]
                  + [(kr, vr, k0, c0, c1) for kr, vr, k0 in new_chunks])
        qq = jnp.concatenate([q0_ref[pl.ds(r0, tq), c0:c1], q1_ref[pl.ds(r0, tq), c0:c1]], axis=0)
        mrun = None
        for ci, (kr, _, k0, a0, a1) in enumerate(chunks):
            s_c = _dot_nt(qq, kr[k0:k0 + KEY_CHUNK, a0:a1])
            s_buf[:, ci * KEY_CHUNK:(ci + 1) * KEY_CHUNK] = s_c
            for h0 in range(0, KEY_CHUNK, LANES):
                part = s_c[:, h0:h0 + LANES]
                mrun = part if mrun is None else jnp.maximum(mrun, part)
        m = jnp.max(mrun, axis=-1, keepdims=True)
        acc = jnp.zeros((2 * tq, HEAD_COLS), F32)
        lrun = jnp.zeros((2 * tq, LANES), F32)
        for ci, (_, vr, k0, a0, a1) in enumerate(chunks):
            p = jnp.exp2(s_buf[:, ci * KEY_CHUNK:(ci + 1) * KEY_CHUNK] - m)
            for h0 in range(0, KEY_CHUNK, LANES):
                lrun = lrun + p[:, h0:h0 + LANES]
            acc = acc + _dot(p.astype(BF16), vr[k0:k0 + KEY_CHUNK, a0:a1])
        r = 1.0 / jnp.sum(lrun, axis=-1, keepdims=True)
        o = acc[:tq] * r[:tq] - acc[tq:] * (r[tq:] * lam_val)
        o = o * lax.rsqrt(jnp.mean(o * o, axis=-1, keepdims=True) + EPS) * sw
        o_ref[pl.ds(r0, tq), c0:c1] = o.astype(BF16)

    n_par = s_scr.shape[0] // heads

    def body(j, carry):
        for hh in range(heads):
            for u in range(n_par):
                block(pl.multiple_of((j * n_par + u) * tq, tq), hh, s_scr.at[hh * n_par + u])
        return carry

    lax.fori_loop(0, n_seq // (tq * n_par), body, 0)


def _attn_call(q0, q1, kvb, lam_l, subln_l, lam_init, n_batch, n_seq, row_off, cache=None):
    blk_off = row_off // n_seq
    tq = min(TQ, n_seq)
    has_cache = cache is not None
    heads = 1 if has_cache else A_HEADS
    n_hsteps = A_HEADS // heads
    cols = heads * HEAD_COLS
    q_spec = pl.BlockSpec((n_seq, cols), lambda b, h: (blk_off + b, h))
    in_specs = [q_spec, q_spec,
                pl.BlockSpec((n_seq, cols), lambda b, h: (blk_off + b, h)),
                pl.BlockSpec((n_seq, cols), lambda b, h: (blk_off + b, n_hsteps + h))]
    args = [q0, q1, kvb, kvb]
    past = 0
    if has_cache:
        ck, cv, layer = cache
        past = ck.shape[2]
        c_spec = pl.BlockSpec((1, 1, past, HEAD_COLS), lambda b, h: (b, layer, 0, h))
        in_specs += [c_spec, c_spec]
        args += [ck, cv]
    n_par = min(Q_PAR, n_seq // tq)
    scratch = [pltpu.VMEM((heads * n_par, 2 * tq, past + n_seq), F32)]
    if has_cache:
        scratch += [pltpu.VMEM((past, HEAD_COLS), BF16), pltpu.VMEM((past, HEAD_COLS), BF16)]
    in_specs += [pl.BlockSpec((4, MAP_DIM), lambda b, h: (0, 0)),
                 pl.BlockSpec((1, HEAD_COLS), lambda b, h: (0, 0))]
    args += [lam_l, subln_l]
    kern = functools.partial(_attn_kernel, has_cache=has_cache, lam_init=float(lam_init),
                             n_seq=n_seq, past=past, tq=tq, heads=heads)
    return pl.pallas_call(
        kern,
        grid=(n_batch, n_hsteps),
        in_specs=in_specs,
        out_specs=pl.BlockSpec((n_seq, cols), lambda b, h: (b, h)),
        out_shape=jax.ShapeDtypeStruct((n_batch * n_seq, A_WIDTH), BF16),
        scratch_shapes=scratch,
        compiler_params=_cparams(2),
        name="attn_cache" if has_cache else "attn",
    )(*args)


def _fourier_kernel(c_ref, s_ref, z_ref, o_ref, *, scale):
    zc = z_ref[:, :C_WIDTH]
    zs = z_ref[:, C_WIDTH:]
    y = _dot(c_ref[...], zc) - _dot(s_ref[...], zs)
    o_ref[...] = (y * scale).astype(BF16)


def _fourier_call(zcs, cos_n, sin_n, n_batch, n_seq, row_off):
    tr = min(FOURIER_TR, n_seq)
    n_f = n_seq // tr
    blk_off = row_off // n_seq
    kern = functools.partial(_fourier_kernel, scale=float((n_seq * C_GROUP_DIM) ** -0.5))
    return pl.pallas_call(
        kern,
        grid=(n_batch, n_f),
        in_specs=[pl.BlockSpec((tr, n_seq), lambda b, f: (f, 0)),
                  pl.BlockSpec((tr, n_seq), lambda b, f: (f, 0)),
                  pl.BlockSpec((n_seq, 2 * C_WIDTH), lambda b, f: (blk_off + b, 0))],
        out_specs=pl.BlockSpec((tr, C_WIDTH), lambda b, f: (b * n_f + f, 0)),
        out_shape=jax.ShapeDtypeStruct((n_batch * n_seq, C_WIDTH), BF16),
        compiler_params=_cparams(2),
        name="fourier",
    )(cos_n, sin_n, zcs)


def _out_kernel(oa_s_ref, oa_p_ref, oc_s_ref, oc_p_ref, u_ref, gv_ref, x_ref, mod_ref,
                sgw_ref, sgb_ref, w_ref, g_ref, rw_hi_ref, rw_lo_ref, rb_ref,
                xo_ref, h2_ref, lg_ref, mix_scr, *, ns_tiles):
    i = pl.program_id(0)
    d = D_MODEL

    @pl.when(i < ns_tiles)
    def _():
        mix_scr[:, :A_WIDTH] = oa_s_ref[...]
        mix_scr[:, A_WIDTH + B_WIDTH:] = oc_s_ref[...]

    @pl.when(i >= ns_tiles)
    def _():
        mix_scr[:, :A_WIDTH] = oa_p_ref[...]
        mix_scr[:, A_WIDTH + B_WIDTH:] = oc_p_ref[...]

    for c in range(x_ref.shape[0] // B_CHUNK):
        r0 = c * B_CHUNK
        for g in range(B_GROUPS):
            c0 = g * (B_WIDTH // B_GROUPS)
            c1 = c0 + B_WIDTH // B_GROUPS
            gate = _dot(sgw_ref[g], gv_ref[r0:r0 + B_CHUNK, c0:c1]) + sgb_ref[:, g:g + 1]
            ob = u_ref[r0:r0 + B_CHUNK, c0:c1].astype(F32) * gate
            mix_scr[r0:r0 + B_CHUNK, A_WIDTH + c0:A_WIDTH + c1] = ob.astype(BF16)

    m = mod_ref[0]
    g1 = m[:, 2 * d:3 * d]
    sh2 = m[:, 3 * d:4 * d]
    sc2 = m[:, 4 * d:5 * d]
    y = _dot(mix_scr[...], w_ref[...])
    xn = x_ref[...] + g1 * y
    xo_ref[...] = xn
    ms = jnp.mean(xn * xn, axis=-1, keepdims=True)
    h2 = (xn * lax.rsqrt(ms + EPS) * g_ref[...]) * (1.0 + sc2) + sh2
    h_hi, h_lo = _split_bf16(h2)
    h2_ref[...] = h_hi
    logits = (_dot_nt(rw_hi_ref[...], h_hi) + _dot_nt(rw_hi_ref[...], h_lo)
              + _dot_nt(rw_lo_ref[...], h_hi))
    lg_ref[...] = logits + rb_ref[...]


def _out_call(oa_s, oa_p, oc_s, oc_p, u, gv, x, mod_l, sgw_b, sgb_t, w_out_b, norm_g,
              rw_hi, rw_lo, rb_col, geo):
    t_tok, d = x.shape
    tm = TM_OUT
    n_tiles = t_tok // tm
    ns_tiles = geo["t_sample"] // tm
    sample_seq = geo["sample_seq"]
    n_sb = geo["n_sample_b"]

    def mod_row(i):
        return (jnp.minimum((i * tm) // sample_seq, n_sb), 0, 0)

    row_blk = lambda i: (i, 0)
    s_blk = lambda i: (jnp.minimum(i, ns_tiles - 1), 0)
    p_blk = lambda i: (jnp.maximum(i - ns_tiles, 0), 0)
    const2 = lambda i: (0, 0)
    kern = functools.partial(_out_kernel, ns_tiles=ns_tiles)
    return pl.pallas_call(
        kern,
        grid=(n_tiles,),
        in_specs=[pl.BlockSpec((tm, A_WIDTH), s_blk),
                  pl.BlockSpec((tm, A_WIDTH), p_blk),
                  pl.BlockSpec((tm, C_WIDTH), s_blk),
                  pl.BlockSpec((tm, C_WIDTH), p_blk),
                  pl.BlockSpec((tm, B_WIDTH), row_blk),
                  pl.BlockSpec((tm, B_WIDTH), row_blk),
                  pl.BlockSpec((tm, d), row_blk),
                  pl.BlockSpec((1, 1, 6 * d), mod_row),
                  pl.BlockSpec((B_GROUPS, B_CHUNK, B_CHUNK), lambda i: (0, 0, 0)),
                  pl.BlockSpec((B_CHUNK, B_GROUPS), const2),
                  pl.BlockSpec((d, d), const2, pipeline_mode=pl.Buffered(1)),
                  pl.BlockSpec((1, d), const2),
                  pl.BlockSpec((N_EXPERTS, d), const2),
                  pl.BlockSpec((N_EXPERTS, d), const2),
                  pl.BlockSpec((N_EXPERTS, 1), const2)],
        out_specs=[pl.BlockSpec((tm, d), row_blk),
                   pl.BlockSpec((tm, d), row_blk),
                   pl.BlockSpec((N_EXPERTS, tm), lambda i: (0, i))],
        out_shape=[jax.ShapeDtypeStruct((t_tok, d), F32),
                   jax.ShapeDtypeStruct((t_tok, d), BF16),
                   jax.ShapeDtypeStruct((N_EXPERTS, t_tok), F32)],
        scratch_shapes=[pltpu.VMEM((tm, d), BF16)],
        compiler_params=_cparams(1),
        name="out",
    )(oa_s, oa_p, oc_s, oc_p, u, gv, x, mod_l, sgw_b, sgb_t, w_out_b, norm_g,
      rw_hi, rw_lo, rb_col)


def _route_kernel(lg_ref, idx_ref, gate_ref, rank_ref, cnt_ref):
    lg = lg_ref[...]
    n_e, tr = lg.shape
    eidx = lax.broadcasted_iota(I32, lg.shape, 0).astype(F32)
    work = lg
    vals = []
    sels = []
    for _ in range(TOP_K):
        mk = jnp.max(work, axis=0, keepdims=True)
        ik = jnp.min(jnp.where(work == mk, eidx, float(n_e)), axis=0, keepdims=True)
        hit = eidx == ik
        work = jnp.where(hit, -jnp.inf, work)
        vals.append(mk)
        sels.append(hit)
    es = [jnp.exp(v - vals[0]) for v in vals]
    inv = 1.0 / (es[0] + es[1] + es[2] + es[3])
    onehot = jnp.where(sels[0] | sels[1] | sels[2] | sels[3], 1.0, 0.0)
    before = (lax.broadcasted_iota(I32, (tr, tr), 0) < lax.broadcasted_iota(I32, (tr, tr), 1))
    upper = jnp.where(before, 1.0, 0.0).astype(BF16)
    prefix = _dot(onehot.astype(BF16), upper)
    for k in range(TOP_K):
        idx_ref[k:k + 1, :] = jnp.sum(jnp.where(sels[k], eidx, 0.0), axis=0, keepdims=True).astype(I32)
        rank_ref[k:k + 1, :] = jnp.sum(jnp.where(sels[k], prefix, 0.0), axis=0, keepdims=True).astype(I32)
        gate_ref[k:k + 1, :] = es[k] * inv
    cnt_ref[0] = jnp.sum(onehot, axis=1, keepdims=True).astype(I32)


def _route_call(logits_t):
    n_e, t_tok = logits_t.shape
    n_tiles = t_tok // TR
    tok_blk = lambda i: (0, i)
    return pl.pallas_call(
        _route_kernel,
        grid=(n_tiles,),
        in_specs=[pl.BlockSpec((n_e, TR), tok_blk)],
        out_specs=[pl.BlockSpec((TOP_K, TR), tok_blk),
                   pl.BlockSpec((TOP_K, TR), tok_blk),
                   pl.BlockSpec((TOP_K, TR), tok_blk),
                   pl.BlockSpec((1, n_e, 1), lambda i: (i, 0, 0))],
        out_shape=[jax.ShapeDtypeStruct((TOP_K, t_tok), I32),
                   jax.ShapeDtypeStruct((TOP_K, t_tok), F32),
                   jax.ShapeDtypeStruct((TOP_K, t_tok), I32),
                   jax.ShapeDtypeStruct((n_tiles, n_e, 1), I32)],
        compiler_params=_cparams(1),
        name="route",
    )(logits_t)


def _rows_local():
    return TOP_K * TR + N_EXPERTS * BF16_SUBLANES


def _plan(counts, t_tok):
    n_tiles = t_tok // TR
    g = BF16_SUBLANES
    rows_loc = _rows_local()
    n_chunk_max = rows_loc // g
    counts = counts.reshape(n_tiles, N_EXPERTS)
    padded = (counts + g - 1) // g * g
    loc_end = jnp.cumsum(padded, axis=1)
    loc_start = loc_end - padded
    tot = jnp.sum(padded, axis=0)
    region = (tot + BM - 1) // BM * BM
    reg_end = jnp.cumsum(region)
    reg_start = reg_end - region
    seg_off = reg_start[None, :] + jnp.cumsum(padded, axis=0) - padded
    n_chunks = loc_end[:, -1] // g
    chunk_row = jnp.arange(n_chunk_max, dtype=I32) * g
    chunk_e = jnp.sum((loc_end[:, None, :] <= chunk_row[None, :, None]).astype(I32), axis=-1)
    chunk_e = jnp.minimum(chunk_e, N_EXPERTS - 1)
    owner = chunk_e[:, :, None] == jnp.arange(N_EXPERTS, dtype=I32)[None, None, :]
    seg_shift = jnp.sum(jnp.where(owner, (seg_off - loc_start)[:, None, :], 0), axis=-1)
    chunk_dst = seg_shift + chunk_row[None, :]
    n_rows_max = (TOP_K * t_tok + n_tiles * N_EXPERTS * (g - 1) + N_EXPERTS * (BM - g))
    n_blocks_max = -(-n_rows_max // BM)
    n_used = reg_end[-1] // BM
    blk_row = jnp.arange(n_blocks_max, dtype=I32) * BM
    blk_e = jnp.sum((reg_end[None, :] <= blk_row[:, None]).astype(I32), axis=-1)
    blk_e = jnp.minimum(blk_e, N_EXPERTS - 1)
    pad_start = reg_start + tot
    pad_chunks = (region - tot) // g
    return dict(loc_start=loc_start.reshape(-1).astype(I32),
                n_chunks=n_chunks.astype(I32),
                chunk_dst=chunk_dst.reshape(-1).astype(I32),
                blk_e=blk_e.astype(I32),
                n_used=n_used.reshape(1).astype(I32),
                pad_start=pad_start.astype(I32),
                pad_chunks=pad_chunks.astype(I32),
                n_blocks_max=n_blocks_max)


def _dispatch_kernel(ls_ref, nch_ref, cdst_ref, pst_ref, pch_ref, nused_ref,
                     h_ref, idx_ref, rank_ref, xs_hbm, loc_scr, zero_scr, sem, zsem,
                     *, n_blocks_max):
    i = pl.program_id(0)
    n_tiles = pl.num_programs(0)
    g = BF16_SUBLANES
    rows_loc = loc_scr.shape[1]
    n_chunk_max = rows_loc // g
    slot = i % 2

    def seg_copy(tile, c, sl):
        src = pl.multiple_of(c * g, g)
        dst = pl.multiple_of(cdst_ref[tile * n_chunk_max + c], g)
        return pltpu.make_async_copy(loc_scr.at[sl, pl.ds(src, g), :],
                                     xs_hbm.at[pl.ds(dst, g), :], sem.at[sl])

    def wait_tile(tile, sl):
        def wait(c, carry):
            seg_copy(tile, 0, sl).wait()
            return carry
        lax.fori_loop(0, nch_ref[tile], wait, 0)

    @pl.when(i >= 2)
    def _():
        wait_tile(i - 2, slot)

    idx = idx_ref[...]
    lrow = rank_ref[...]
    for e in range(N_EXPERTS):
        lrow = lrow + jnp.where(idx == e, ls_ref[i * N_EXPERTS + e], 0)

    def sort_slab(s, carry):
        r0 = pl.multiple_of(s * SLAB, SLAB)
        riota = lax.broadcasted_iota(I32, (SLAB, TR), 0) + r0
        sel = jnp.where(riota == lrow[0:1, :], 1.0, 0.0)
        for k in range(1, TOP_K):
            sel = jnp.where(riota == lrow[k:k + 1, :], 1.0, sel)
        loc_scr[slot, pl.ds(r0, SLAB), :] = _dot(sel.astype(BF16), h_ref[...]).astype(BF16)
        return carry

    lax.fori_loop(0, rows_loc // SLAB, sort_slab, 0)

    def start(c, carry):
        seg_copy(i, c, slot).start()
        return carry

    lax.fori_loop(0, nch_ref[i], start, 0)

    @pl.when(i == n_tiles - 1)
    def _():
        @pl.when(i >= 1)
        def _():
            wait_tile(i - 1, 1 - slot)

        wait_tile(i, slot)
        zero_scr[...] = jnp.zeros_like(zero_scr)

        def pad_copy(row):
            row = pl.multiple_of(row, g)
            return pltpu.make_async_copy(zero_scr.at[pl.ds(0, g), :], xs_hbm.at[pl.ds(row, g), :], zsem)

        def blk_copy(b):
            row = pl.multiple_of(b * BM, BM)
            return pltpu.make_async_copy(zero_scr, xs_hbm.at[pl.ds(row, BM), :], zsem)

        for e in range(N_EXPERTS):
            def pstart(j, carry, e=e):
                pad_copy(pst_ref[e] + j * g).start()
                return carry

            def pwait(j, carry, e=e):
                pad_copy(pst_ref[e] + j * g).wait()
                return carry

            lax.fori_loop(0, pch_ref[e], pstart, 0)
            lax.fori_loop(0, pch_ref[e], pwait, 0)

        def bstart(b, carry):
            blk_copy(b).start()
            return carry

        def bwait(b, carry):
            blk_copy(b).wait()
            return carry

        lax.fori_loop(nused_ref[0], n_blocks_max, bstart, 0)
        lax.fori_loop(nused_ref[0], n_blocks_max, bwait, 0)


def _dispatch_call(plan, h2, idx_t, rank_t):
    t_tok, d = h2.shape
    n_tiles = t_tok // TR
    rows_loc = _rows_local()
    n_blocks_max = plan["n_blocks_max"]
    kern = functools.partial(_dispatch_kernel, n_blocks_max=n_blocks_max)
    grid_spec = pltpu.PrefetchScalarGridSpec(
        num_scalar_prefetch=6,
        grid=(n_tiles,),
        in_specs=[pl.BlockSpec((TR, d), lambda i, *_: (i, 0)),
                  pl.BlockSpec((TOP_K, TR), lambda i, *_: (0, i)),
                  pl.BlockSpec((TOP_K, TR), lambda i, *_: (0, i))],
        out_specs=pl.BlockSpec(memory_space=pl.ANY),
        scratch_shapes=[pltpu.VMEM((2, rows_loc, d), BF16),
                        pltpu.VMEM((BM, d), BF16),
                        pltpu.SemaphoreType.DMA((2,)),
                        pltpu.SemaphoreType.DMA(())])
    return pl.pallas_call(
        kern,
        grid_spec=grid_spec,
        out_shape=jax.ShapeDtypeStruct((n_blocks_max * BM, d), BF16),
        compiler_params=_cparams(1),
        name="dispatch",
    )(plan["loc_start"], plan["n_chunks"], plan["chunk_dst"], plan["pad_start"],
      plan["pad_chunks"], plan["n_used"], h2, idx_t, rank_t)


def _expert_kernel(be_ref, nused_ref, x_ref, w1_ref, b1_ref, w2_ref, b2_ref, y_ref):
    b = pl.program_id(0)

    @pl.when(b < nused_ref[0])
    def _():
        gu = _dot(x_ref[...], w1_ref[0]) + b1_ref[0]
        gate = jnp.minimum(gu[:, :D_FF], SWIGLU_LIMIT)
        up = jnp.clip(gu[:, D_FF:], -SWIGLU_LIMIT, SWIGLU_LIMIT)
        act = (up + 1.0) * gate * (1.0 / (1.0 + jnp.exp(-SWIGLU_ALPHA * gate)))
        y = _dot(act.astype(BF16), w2_ref[0]) + b2_ref[0]
        y_ref[...] = y.astype(BF16)

    @pl.when(b >= nused_ref[0])
    def _():
        y_ref[...] = jnp.zeros_like(y_ref)


def _expert_call(plan, xs, w1_all, b1_all, w2_all, b2_all, layer):
    n_rows, d = xs.shape
    n_blocks = n_rows // BM
    e0 = layer * N_EXPERTS
    w_blk = lambda b, be, nu: (e0 + be[b], 0, 0)
    grid_spec = pltpu.PrefetchScalarGridSpec(
        num_scalar_prefetch=2,
        grid=(n_blocks,),
        in_specs=[pl.BlockSpec((BM, d), lambda b, be, nu: (jnp.minimum(b, nu[0] - 1), 0)),
                  pl.BlockSpec((1, d, 2 * D_FF), w_blk),
                  pl.BlockSpec((1, 1, 2 * D_FF), w_blk),
                  pl.BlockSpec((1, D_FF, d), w_blk),
                  pl.BlockSpec((1, 1, d), w_blk)],
        out_specs=pl.BlockSpec((BM, d), lambda b, be, nu: (b, 0)))
    return pl.pallas_call(
        _expert_kernel,
        grid_spec=grid_spec,
        out_shape=jax.ShapeDtypeStruct((n_rows, d), BF16),
        compiler_params=_cparams(1),
        name="experts",
    )(plan["blk_e"], plan["n_used"], xs, w1_all, b1_all, w2_all, b2_all)


def _combine_kernel(ls_ref, nch_ref, cdst_ref, x_ref, mod_ref, idx_ref, rank_ref, gate_ref,
                    fin_ref, ys_hbm, *rest, final, ns_tiles):
    if final:
        os_ref, op_ref, loc_scr, w_scr, sem = rest
    else:
        o_ref, loc_scr, w_scr, sem = rest
    i = pl.program_id(0)
    n_tiles = pl.num_programs(0)
    g = BF16_SUBLANES
    d = D_MODEL
    rows_loc = loc_scr.shape[1]
    n_chunk_max = rows_loc // g
    slot = i % 2

    def seg_copy(tile, c, sl):
        dst = pl.multiple_of(c * g, g)
        src = pl.multiple_of(cdst_ref[tile * n_chunk_max + c], g)
        return pltpu.make_async_copy(ys_hbm.at[pl.ds(src, g), :],
                                     loc_scr.at[sl, pl.ds(dst, g), :], sem.at[sl])

    def fetch_tile(tile, sl):
        def start(c, carry):
            seg_copy(tile, c, sl).start()
            return carry
        lax.fori_loop(0, nch_ref[tile], start, 0)

    @pl.when(i == 0)
    def _():
        loc_scr[...] = jnp.zeros_like(loc_scr)
        fetch_tile(0, 0)

    @pl.when(i + 1 < n_tiles)
    def _():
        fetch_tile(i + 1, 1 - slot)

    idx = idx_ref[...]
    lrow = rank_ref[...]
    for e in range(N_EXPERTS):
        lrow = lrow + jnp.where(idx == e, ls_ref[i * N_EXPERTS + e], 0)
    gates = gate_ref[...]
    for s in range(rows_loc // SLAB):
        ciota = lax.broadcasted_iota(I32, (TR, SLAB), 1) + s * SLAB
        wsel = jnp.where(ciota == lrow[:, 0:1], gates[:, 0:1], 0.0)
        for k in range(1, TOP_K):
            wsel = jnp.where(ciota == lrow[:, k:k + 1], gates[:, k:k + 1], wsel)
        w_scr[:, s * SLAB:(s + 1) * SLAB] = wsel.astype(BF16)

    def wait(c, carry):
        seg_copy(i, 0, slot).wait()
        return carry

    lax.fori_loop(0, nch_ref[i], wait, 0)

    y = _dot(w_scr[...], loc_scr[slot])
    g2 = mod_ref[0][:, 5 * d:6 * d]
    xn = x_ref[...] + g2 * y
    if final:
        ms = jnp.mean(xn * xn, axis=-1, keepdims=True)
        xn = xn * lax.rsqrt(ms + EPS) * fin_ref[...]

        @pl.when(i < ns_tiles)
        def _():
            os_ref[...] = xn

        @pl.when(i >= ns_tiles)
        def _():
            op_ref[...] = xn
    else:
        o_ref[...] = xn


def _combine_call(plan, x, mod_l, idx_c, rank_c, gate_c, final_g, ys, geo, final):
    t_tok, d = x.shape
    n_tiles = t_tok // TR
    rows_loc = _rows_local()
    sample_seq = geo["sample_seq"]
    n_sb = geo["n_sample_b"]

    def mod_row(i, *_):
        return (jnp.minimum((i * TR) // sample_seq, n_sb), 0, 0)

    row_blk = lambda i, *_: (i, 0)
    ns_tiles = geo["t_sample"] // TR
    if final:
        out_specs = [pl.BlockSpec((TR, d), lambda i, *_: (jnp.minimum(i, ns_tiles - 1), 0)),
                     pl.BlockSpec((TR, d), lambda i, *_: (jnp.maximum(i - ns_tiles, 0), 0))]
        out_shape = [jax.ShapeDtypeStruct((geo["t_sample"], d), F32),
                     jax.ShapeDtypeStruct((t_tok - geo["t_sample"], d), F32)]
    else:
        out_specs = pl.BlockSpec((TR, d), row_blk)
        out_shape = jax.ShapeDtypeStruct((t_tok, d), F32)
    kern = functools.partial(_combine_kernel, final=final, ns_tiles=ns_tiles)
    grid_spec = pltpu.PrefetchScalarGridSpec(
        num_scalar_prefetch=3,
        grid=(n_tiles,),
        in_specs=[pl.BlockSpec((TR, d), row_blk),
                  pl.BlockSpec((1, 1, 6 * d), mod_row),
                  pl.BlockSpec((TR, TOP_K), row_blk),
                  pl.BlockSpec((TR, TOP_K), row_blk),
                  pl.BlockSpec((TR, TOP_K), row_blk),
                  pl.BlockSpec((1, d), lambda i, *_: (0, 0)),
                  pl.BlockSpec(memory_space=pl.ANY)],
        out_specs=out_specs,
        scratch_shapes=[pltpu.VMEM((2, rows_loc, d), BF16),
                        pltpu.VMEM((TR, rows_loc), BF16),
                        pltpu.SemaphoreType.DMA((2,))])
    return pl.pallas_call(
        kern,
        grid_spec=grid_spec,
        out_shape=out_shape,
        compiler_params=_cparams(1),
        name="combine_final" if final else "combine",
    )(plan["loc_start"], plan["n_chunks"], plan["chunk_dst"], x, mod_l, idx_c, rank_c, gate_c,
      final_g, ys)


def _rope_tables(n_tok):
    rows = n_tok // GRID_W
    row = jnp.repeat(jnp.arange(rows, dtype=F32), GRID_W)
    col = jnp.tile(jnp.arange(GRID_W, dtype=F32), rows)
    pairs = ROPE_HALF // 2
    freqs = ROPE_THETA ** (-jnp.arange(pairs, dtype=F32) / pairs)
    ang = jnp.concatenate([row[:, None] * freqs, col[:, None] * freqs], axis=-1)
    cos, sin = jnp.cos(ang), jnp.sin(ang)
    cos_t = jnp.concatenate([cos, cos, cos, cos], axis=-1)
    sin_t = jnp.concatenate([-sin, sin, -sin, sin], axis=-1)
    return cos_t, sin_t


def _dft_tables(n):
    k = jnp.arange(n, dtype=I32)
    ang = ((k[:, None] * k[None, :]) % n).astype(F32) * (2.0 * math.pi / n)
    return jnp.cos(ang), jnp.sin(ang)


def _forward(x_prompt, x_sample, cache_k, cache_v, c, c_ctx, w_ada, b_ada, norm_mix, norm_ffn,
             w_in, lam, subln_w, sgu_w, sgu_b, w_out, router_w, router_b, moe_w1, moe_b1,
             moe_w2, moe_b2, final_norm):
    n_pb, p_seq, d = x_prompt.shape
    n_sb, s_seq, _ = x_sample.shape
    depth = w_ada.shape[0]
    past = cache_k.shape[2]
    t_sample = n_sb * s_seq
    t_prompt = n_pb * p_seq
    t_tok = t_sample + t_prompt
    geo = dict(t_sample=t_sample, sample_seq=s_seq, n_sample_b=n_sb)

    x = jnp.concatenate([x_sample.reshape(t_sample, d), x_prompt.reshape(t_prompt, d)], axis=0)
    cond = jnp.zeros((COND_ROWS, d), F32).at[:n_sb].set(c).at[n_sb].set(c_ctx)
    mod = _ada_call(cond, w_ada, b_ada)

    cos_t, sin_t = _rope_tables(s_seq)
    cc, sc = _dft_tables(C_GROUP_DIM)
    cs_chan = jnp.concatenate([cc, sc], axis=1).astype(BF16)
    cos_s, sin_s = [a.astype(BF16) for a in _dft_tables(s_seq)]
    cos_p, sin_p = [a.astype(BF16) for a in _dft_tables(p_seq)]
    ck = cache_k.reshape(n_sb, depth, past, A_WIDTH)
    cv = cache_v.reshape(n_sb, depth, past, A_WIDTH)

    n_e = moe_w1.shape[1]
    w1_all = moe_w1.astype(BF16).reshape(depth * n_e, d, 2 * D_FF)
    w2_all = moe_w2.astype(BF16).reshape(depth * n_e, D_FF, d)
    b1_all = moe_b1.reshape(depth * n_e, 1, 2 * D_FF)
    b2_all = moe_b2.reshape(depth * n_e, 1, d)

    new_k, new_v = [], []
    for l in range(depth):
        lam_init = 0.8 - 0.6 * math.exp(-0.3 * l)
        mod_l = mod[l].reshape(COND_ROWS, 1, 6 * d)
        q0, q1, kvb, kvf, u, gv, zcs = _proj_call(
            x, mod_l, norm_mix[l].reshape(1, d), w_in[l].astype(BF16), cs_chan, cos_t, sin_t, geo)
        new_k.append(kvf[:, :A_WIDTH].reshape(n_pb, p_seq, A_HEADS, HEAD_COLS))
        new_v.append(kvf[:, A_WIDTH:].reshape(n_pb, p_seq, A_HEADS, HEAD_COLS))
        subln_l = subln_w[l].reshape(1, HEAD_COLS)
        oa_s = _attn_call(q0, q1, kvb, lam[l], subln_l, lam_init, n_sb, s_seq, 0, cache=(ck, cv, l))
        oa_p = _attn_call(q0, q1, kvb, lam[l], subln_l, lam_init, n_pb, p_seq, t_sample)
        oc_s = _fourier_call(zcs, cos_s, sin_s, n_sb, s_seq, 0)
        oc_p = _fourier_call(zcs, cos_p, sin_p, n_pb, p_seq, t_sample)
        rw_hi, rw_lo = _split_bf16(router_w[l].T)
        x, h2, logits_t = _out_call(
            oa_s, oa_p, oc_s, oc_p, u, gv, x, mod_l, sgu_w[l].astype(BF16), sgu_b[l].T,
            w_out[l].astype(BF16), norm_ffn[l].reshape(1, d), rw_hi, rw_lo,
            router_b[l].reshape(N_EXPERTS, 1), geo)
        idx_t, gate_t, rank_t, counts = _route_call(logits_t)
        plan = _plan(counts, t_tok)
        xs = _dispatch_call(plan, h2, idx_t, rank_t)
        ys = _expert_call(plan, xs, w1_all, b1_all, w2_all, b2_all, l)
        x = _combine_call(plan, x, mod_l, idx_t.T, rank_t.T, gate_t.T, final_norm.reshape(1, d),
                          ys, geo, final=(l == depth - 1))

    y_sample = x[0].reshape(n_sb, s_seq, d)
    y_prompt = x[1].reshape(n_pb, p_seq, d)
    return y_prompt, y_sample, jnp.stack(new_k, axis=1), jnp.stack(new_v, axis=1)


def kernel(x_prompt, x_sample, cache_k, cache_v, c, c_ctx, w_ada, b_ada, norm_mix, norm_ffn, w_in,
           lam, subln_w, sgu_w, sgu_b, w_out, router_w, router_b, moe_w1, moe_b1, moe_w2, moe_b2,
           final_norm):
    return _forward(x_prompt, x_sample, cache_k, cache_v, c, c_ctx, w_ada, b_ada, norm_mix,
                    norm_ffn, w_in, lam, subln_w, sgu_w, sgu_b, w_out, router_w, router_b,
                    moe_w1, moe_b1, moe_w2, moe_b2, final_norm)
```
